```python
import math
import jax, jax.numpy as jnp
from jax import lax
import numpy as np

D_MODEL = 1024
BATCH = 8
SEQ = 2048
DEPTH = 4

HEAD_DIM = 64
N_MOBA_HEADS = D_MODEL // (2 * HEAD_DIM)
N_FOX_HEADS = D_MODEL // (2 * HEAD_DIM)
MOBA_WIDTH = N_MOBA_HEADS * HEAD_DIM
FOX_WIDTH = N_FOX_HEADS * HEAD_DIM
MIX_WIDTH = MOBA_WIDTH + FOX_WIDTH
IN_WIDTH = 3 * MIX_WIDTH + N_FOX_HEADS
MOBA_BLOCK = 256
MOBA_TOPK = 3
MOBA_Q_CHUNK = 32
FOX_Q_BLOCK = 128
NUM_BUCKETS = 32
MAX_DISTANCE = 1024
D_FF = -(-(8 * D_MODEL) // (3 * 256)) * 256
RMS_EPS = 1e-6

kernel_name = "hymba_style_moba_fox_hybrid"


def rmsnorm(x, g):
    xf = x.astype(jnp.float32)
    y = xf * lax.rsqrt(jnp.mean(xf * xf, axis=-1, keepdims=True) + RMS_EPS)
    return (y * g.astype(jnp.float32)).astype(x.dtype)


def t5_bucket(dist):
    max_exact = NUM_BUCKETS // 2
    is_small = dist < max_exact
    d = jnp.maximum(dist, 1).astype(jnp.float32)
    large = max_exact + (jnp.log(d / max_exact) / math.log(MAX_DISTANCE / max_exact)
                         * (NUM_BUCKETS - max_exact)).astype(jnp.int32)
    large = jnp.minimum(large, NUM_BUCKETS - 1)
    return jnp.where(is_small, dist, large)


def split_heads(t, n_heads):
    b, s, _ = t.shape
    return t.reshape(b, s, n_heads, HEAD_DIM).transpose(0, 2, 1, 3)


def moba_attention(q, k, v, rel_bias):
    B, H, S, Dh = q.shape
    nb = -(-S // MOBA_BLOCK)
    pad = nb * MOBA_BLOCK - S
    kp = jnp.pad(k, ((0, 0), (0, 0), (0, pad), (0, 0)))
    vp = jnp.pad(v, ((0, 0), (0, 0), (0, pad), (0, 0)))
    kb = kp.reshape(B, H, nb, MOBA_BLOCK, Dh)
    vb = vp.reshape(B, H, nb, MOBA_BLOCK, Dh)
    kmean = jnp.mean(kb.astype(jnp.float32), axis=3)
    qblk = jnp.arange(S) // MOBA_BLOCK
    gate = jnp.einsum('bhsd,bhnd->bhsn', q.astype(jnp.float32), kmean)
    past = jnp.arange(nb)[None, :] < qblk[:, None]
    gate = jnp.where(past[None, None], gate, -jnp.inf)
    topk = min(MOBA_TOPK, nb)
    _, gidx = lax.top_k(gate, topk)
    scale = HEAD_DIM ** -0.5
    table_t = rel_bias.T
    bi = jnp.arange(B)[:, None, None, None]
    hi = jnp.arange(H)[None, :, None, None]
    n_chunks = S // MOBA_Q_CHUNK

    def chunk(c):
        t0 = c * MOBA_Q_CHUNK
        cur = t0 // MOBA_BLOCK
        tq = t0 + jnp.arange(MOBA_Q_CHUNK)
        qc = lax.dynamic_slice_in_dim(q, t0, MOBA_Q_CHUNK, axis=2)
        ic = lax.dynamic_slice_in_dim(gidx, t0, MOBA_Q_CHUNK, axis=2)
        ksel = kb[bi, hi, ic]
        vsel = vb[bi, hi, ic]
        s_sel = jnp.einsum('bhqd,bhqkld->bhqkl', qc, ksel).astype(jnp.float32) * scale
        kpos = ic[..., None] * MOBA_BLOCK + jnp.arange(MOBA_BLOCK)
        dist = jnp.maximum(tq[None, None, :, None, None] - kpos, 0)
        s_sel = s_sel + table_t[hi[..., None], t5_bucket(dist)].astype(jnp.float32)
        valid = (jnp.arange(topk) < cur)[None, None, None, :, None]
        s_sel = jnp.where(valid, s_sel, -jnp.inf)
        s_sel = s_sel.reshape(B, H, MOBA_Q_CHUNK, topk * MOBA_BLOCK)
        kcur = lax.dynamic_slice_in_dim(kp, cur * MOBA_BLOCK, MOBA_BLOCK, axis=2)
        vcur = lax.dynamic_slice_in_dim(vp, cur * MOBA_BLOCK, MOBA_BLOCK, axis=2)
        s_own = jnp.einsum('bhqd,bhld->bhql', qc, kcur).astype(jnp.float32) * scale
        kpos_own = cur * MOBA_BLOCK + jnp.arange(MOBA_BLOCK)
        d_own = tq[:, None] - kpos_own[None, :]
        b_own = rel_bias[t5_bucket(jnp.maximum(d_own, 0))].transpose(2, 0, 1)
        s_own = s_own + b_own[None].astype(jnp.float32)
        s_own = jnp.where((d_own >= 0)[None, None], s_own, -jnp.inf)
        p = jax.nn.softmax(jnp.concatenate([s_sel, s_own], axis=-1), axis=-1).astype(v.dtype)
        p_sel = p[..., :topk * MOBA_BLOCK].reshape(B, H, MOBA_Q_CHUNK, topk, MOBA_BLOCK)
        p_own = p[..., topk * MOBA_BLOCK:]
        return (jnp.einsum('bhqkl,bhqkld->bhqd', p_sel, vsel)
                + jnp.einsum('bhql,bhld->bhqd', p_own, vcur))

    out = lax.map(chunk, jnp.arange(n_chunks))
    return out.transpose(1, 2, 0, 3, 4).reshape(B, H, S, Dh)


def fox_attention(q, k, v, log_f):
    B, H, S, Dh = q.shape
    c = jnp.cumsum(log_f, axis=-1)
    scale = HEAD_DIM ** -0.5
    kpos = jnp.arange(S)

    def block(i):
        t0 = i * FOX_Q_BLOCK
        qi = lax.dynamic_slice_in_dim(q, t0, FOX_Q_BLOCK, axis=2)
        ci = lax.dynamic_slice_in_dim(c, t0, FOX_Q_BLOCK, axis=2)
        s = (jnp.einsum('bhqd,bhkd->bhqk', qi, k).astype(jnp.float32) * scale
             + ci[..., :, None] - c[:, :, None, :])
        tq = t0 + jnp.arange(FOX_Q_BLOCK)
        s = jnp.where((kpos[None, :] <= tq[:, None])[None, None], s, -jnp.inf)
        p = jax.nn.softmax(s, axis=-1).astype(v.dtype)
        return jnp.einsum('bhqk,bhkd->bhqd', p, v)

    out = lax.map(block, jnp.arange(S // FOX_Q_BLOCK))
    return out.transpose(1, 2, 0, 3, 4).reshape(B, H, S, Dh)


def setup_inputs(seed: int = 0) -> dict:
    key = jax.random.key(seed)
    ks = jax.random.split(key, 11)
    f32 = jnp.float32
    x = jax.random.normal(ks[0], (BATCH, SEQ, D_MODEL), f32)
    w_in = jax.random.normal(ks[1], (DEPTH, D_MODEL, IN_WIDTH), f32) * D_MODEL ** -0.5
    b_f = jax.random.uniform(ks[2], (DEPTH, N_FOX_HEADS), f32, minval=1.0, maxval=4.0)
    w_o = jax.random.normal(ks[3], (DEPTH, MIX_WIDTH, D_MODEL), f32) * MIX_WIDTH ** -0.5
    g_attn = 1.0 + 0.05 * jax.random.normal(ks[4], (DEPTH, D_MODEL), f32)
    w_gu = jax.random.normal(ks[5], (DEPTH, D_MODEL, 2 * D_FF), f32) * D_MODEL ** -0.5
    w_down = jax.random.normal(ks[6], (DEPTH, D_FF, D_MODEL), f32) * D_FF ** -0.5
    g_ffn = 1.0 + 0.05 * jax.random.normal(ks[7], (DEPTH, D_MODEL), f32)
    rel_bias = 0.5 * jax.random.normal(ks[8], (NUM_BUCKETS, N_MOBA_HEADS), f32)
    g_final = 1.0 + 0.05 * jax.random.normal(ks[9], (D_MODEL,), f32)
    return {"x": x, "w_in": w_in, "b_f": b_f, "w_o": w_o, "g_attn": g_attn,
            "w_gu": w_gu, "w_down": w_down, "g_ffn": g_ffn,
            "rel_bias": rel_bias, "g_final": g_final}


def reference(x, w_in, b_f, w_o, g_attn, w_gu, w_down, g_ffn, rel_bias, g_final):
    M, F = MOBA_WIDTH, FOX_WIDTH
    for layer in range(DEPTH):
        h = rmsnorm(x, g_attn[layer])
        proj = jnp.einsum('bsd,de->bse', h, w_in[layer])
        q_m = split_heads(proj[..., 0:M], N_MOBA_HEADS)
        k_m = split_heads(proj[..., M:2 * M], N_MOBA_HEADS)
        v_m = split_heads(proj[..., 2 * M:3 * M], N_MOBA_HEADS)
        o0 = 3 * M
        q_f = split_heads(proj[..., o0:o0 + F], N_FOX_HEADS)
        k_f = split_heads(proj[..., o0 + F:o0 + 2 * F], N_FOX_HEADS)
        v_f = split_heads(proj[..., o0 + 2 * F:o0 + 3 * F], N_FOX_HEADS)
        f_logit = proj[..., o0 + 3 * F:].astype(jnp.float32) + b_f[layer].astype(jnp.float32)
        log_f = jax.nn.log_sigmoid(f_logit).transpose(0, 2, 1)
        y_m = moba_attention(q_m, k_m, v_m, rel_bias)
        y_f = fox_attention(q_f, k_f, v_f, log_f)
        y = jnp.concatenate([y_m, y_f], axis=1)
        B_, H_, S_, _ = y.shape
        y = y.transpose(0, 2, 1, 3).reshape(B_, S_, H_ * HEAD_DIM)
        x = x + jnp.einsum('bse,ed->bsd', y, w_o[layer])
        h = rmsnorm(x, g_ffn[layer])
        gu = jnp.einsum('bsd,df->bsf', h, w_gu[layer])
        x = x + jnp.einsum('bsf,fd->bsd', jax.nn.silu(gu[..., :D_FF]) * gu[..., D_FF:], w_down[layer])
    return rmsnorm(x, g_final)
```

```python
import functools
import math

import jax
import jax.numpy as jnp
from jax import lax
from jax.experimental import pallas as pl
from jax.experimental.pallas import tpu as pltpu

D_MODEL = 1024
HEAD_DIM = 64
N_HEADS = 8
GROUP_WIDTH = N_HEADS * HEAD_DIM
QKV_WIDTH = 6 * GROUP_WIDTH
BLOCK = 256
TOPK = 3
NUM_BUCKETS = 32
MAX_DISTANCE = 1024
D_FF = 2816
RMS_EPS = 1e-6
SCALE = HEAD_DIM ** -0.5
NEG = -1e30

LANES = 128
HEADS_PER_STEP = LANES // HEAD_DIM
N_PAIRS = N_HEADS // HEADS_PER_STEP
FF_CHUNK = 256
TOKEN_TILE = 512
VMEM_LIMIT = 56 * 1024 * 1024

_NT = (((1,), (1,)), ((), ()))


def _rms(x, g):
    return x * lax.rsqrt(jnp.mean(x * x, axis=-1, keepdims=True) + RMS_EPS) * g


def _inproj_kernel(x_ref, g_ref, w_ref, wf_ref, qkv_ref, flog_ref):
    h = _rms(x_ref[...], g_ref[...]).astype(jnp.bfloat16)
    qkv_ref[...] = jnp.dot(h, w_ref[...], preferred_element_type=jnp.float32).astype(jnp.bfloat16)
    fl = lax.dot_general(wf_ref[...], h, _NT, preferred_element_type=jnp.float32)
    flog_ref[...] = fl[:N_HEADS]


def _inproj(x2d, g, w, wf):
    t = x2d.shape[0]
    return pl.pallas_call(
        _inproj_kernel,
        grid=(t // TOKEN_TILE,),
        in_specs=[
            pl.BlockSpec((TOKEN_TILE, D_MODEL), lambda i: (i, 0)),
            pl.BlockSpec((1, D_MODEL), lambda i: (0, 0)),
            pl.BlockSpec((D_MODEL, QKV_WIDTH), lambda i: (0, 0)),
            pl.BlockSpec((2 * N_HEADS, D_MODEL), lambda i: (0, 0)),
        ],
        out_specs=[
            pl.BlockSpec((TOKEN_TILE, QKV_WIDTH), lambda i: (i, 0)),
            pl.BlockSpec((N_HEADS, TOKEN_TILE), lambda i: (0, i)),
        ],
        out_shape=[
            jax.ShapeDtypeStruct((t, QKV_WIDTH), jnp.bfloat16),
            jax.ShapeDtypeStruct((N_HEADS, t), jnp.float32),
        ],
        compiler_params=pltpu.CompilerParams(
            dimension_semantics=("arbitrary",), vmem_limit_bytes=VMEM_LIMIT),
        name="inproj",
    )(x2d, g, w, wf)


def _bias_kernel(relt_ref, bkt_ref, out_ref, *, seq):
    h = pl.program_id(0)
    bkt = bkt_ref[...]
    w = jnp.zeros(bkt.shape, jnp.float32)
    for b in range(NUM_BUCKETS):
        w = jnp.where(bkt == b, relt_ref[h, b], w)
    x = jnp.broadcast_to(w, (BLOCK, bkt.shape[1]))
    y = pltpu.roll(x, 0, 1, stride=1, stride_axis=0)
    out_ref[0] = y[:, BLOCK:]


def _t5_bucket(dist):
    max_exact = NUM_BUCKETS // 2
    is_small = dist < max_exact
    d = jnp.maximum(dist, 1).astype(jnp.float32)
    large = max_exact + (jnp.log(d / max_exact) / math.log(MAX_DISTANCE / max_exact)
                         * (NUM_BUCKETS - max_exact)).astype(jnp.int32)
    large = jnp.minimum(large, NUM_BUCKETS - 1)
    return jnp.where(is_small, dist, large)


def _bias_strips(rel_bias, seq):
    n = jnp.arange(seq + BLOCK, dtype=jnp.int32)
    bkt = _t5_bucket(jnp.clip(seq - n, 0, seq - 1))[None, :]
    return pl.pallas_call(
        functools.partial(_bias_kernel, seq=seq),
        grid=(N_HEADS,),
        in_specs=[
            pl.BlockSpec(memory_space=pltpu.SMEM),
            pl.BlockSpec((1, seq + BLOCK), lambda h: (0, 0)),
        ],
        out_specs=pl.BlockSpec((1, BLOCK, seq), lambda h: (h, 0, 0)),
        out_shape=jax.ShapeDtypeStruct((N_HEADS, BLOCK, seq), jnp.float32),
        compiler_params=pltpu.CompilerParams(dimension_semantics=("arbitrary",)),
        name="moba_bias",
    )(rel_bias.T, bkt)


def _head_mask(hh):
    lane = lax.broadcasted_iota(jnp.int32, (1, LANES), 1)
    return (lane >= hh * HEAD_DIM) & (lane < (hh + 1) * HEAD_DIM)


def _causal_tile():
    row = lax.broadcasted_iota(jnp.int32, (BLOCK, BLOCK), 0)
    col = lax.broadcasted_iota(jnp.int32, (BLOCK, BLOCK), 1)
    return row >= col


def _softmax_pv(s, v):
    m = jnp.max(s, axis=-1, keepdims=True)
    e = jnp.exp(s - m)
    p = e * (1.0 / jnp.sum(e, axis=-1, keepdims=True))
    return jnp.dot(p.astype(jnp.bfloat16), v, preferred_element_type=jnp.float32)


def _moba_kernel(q_ref, k_ref, v_ref, bias_ref, o_ref, *, seq):
    nb = seq // BLOCK
    k = k_ref[0]
    v = v_ref[0]
    kmean = jnp.sum(k.astype(jnp.float32).reshape(nb, BLOCK, LANES), axis=1) * (1.0 / BLOCK)
    kmean = jnp.concatenate([kmean, jnp.zeros((LANES - nb, LANES), jnp.float32)], axis=0)
    causal = _causal_tile()
    blk_lane = lax.broadcasted_iota(jnp.int32, (BLOCK, LANES), 1)
    for i in range(nb):
        n = (i + 1) * BLOCK
        q = q_ref[0, i * BLOCK:(i + 1) * BLOCK, :]
        outs = []
        for hh in range(HEADS_PER_STEP):
            hmask = _head_mask(hh)
            qm = jnp.where(hmask, q, jnp.zeros_like(q))
            s = lax.dot_general(qm, k[:n], _NT, preferred_element_type=jnp.float32)
            s = s * SCALE + bias_ref[hh, :, (nb - 1 - i) * BLOCK:]
            pieces = []
            if i > TOPK:
                gate = lax.dot_general(qm.astype(jnp.float32), kmean, _NT,
                                       precision=lax.Precision.HIGHEST,
                                       preferred_element_type=jnp.float32)
                rank = jnp.zeros((BLOCK, LANES), jnp.int32)
                for jp in range(i):
                    col = gate[:, jp:jp + 1]
                    beats = (col > gate) | ((col == gate) & (blk_lane > jp))
                    rank = rank + beats.astype(jnp.int32)
                pen = jnp.where(rank < TOPK, 0.0, NEG)
                for j in range(i):
                    pieces.append(s[:, j * BLOCK:(j + 1) * BLOCK] + pen[:, j:j + 1])
            elif i > 0:
                pieces.append(s[:, :i * BLOCK])
            pieces.append(jnp.where(causal, s[:, i * BLOCK:], NEG))
            s = jnp.concatenate(pieces, axis=1) if len(pieces) > 1 else pieces[0]
            outs.append((hmask, _softmax_pv(s, v[:n])))
        o = jnp.where(outs[0][0], outs[0][1], outs[1][1])
        o_ref[0, i * BLOCK:(i + 1) * BLOCK, :] = o.astype(o_ref.dtype)


def _moba(qkv3, bias):
    b, seq, _ = qkv3.shape
    blk = lambda off: pl.BlockSpec((1, seq, LANES), lambda p, bi, off=off: (bi, 0, off + p))
    return pl.pallas_call(
        functools.partial(_moba_kernel, seq=seq),
        grid=(N_PAIRS, b),
        in_specs=[
            blk(0), blk(N_PAIRS), blk(2 * N_PAIRS),
            pl.BlockSpec((HEADS_PER_STEP, BLOCK, seq), lambda p, bi: (p, 0, 0)),
        ],
        out_specs=pl.BlockSpec((1, seq, LANES), lambda p, bi: (bi, 0, p)),
        out_shape=jax.ShapeDtypeStruct((b, seq, GROUP_WIDTH), jnp.bfloat16),
        compiler_params=pltpu.CompilerParams(
            dimension_semantics=("arbitrary", "arbitrary"), vmem_limit_bytes=VMEM_LIMIT),
        name="moba_attn",
    )(qkv3, qkv3, qkv3, bias)


def _fox_kernel(q_ref, k_ref, v_ref, flog_ref, bf_ref, o_ref, c_sc, *, seq):
    nb = seq // BLOCK
    p = pl.program_id(1)

    @pl.when(p == 0)
    def _():
        z = flog_ref[...] + bf_ref[...]
        logf = -(jnp.maximum(-z, 0.0) + jnp.log1p(jnp.exp(-jnp.abs(z))))
        row = lax.broadcasted_iota(jnp.int32, (BLOCK, BLOCK), 0)
        col = lax.broadcasted_iota(jnp.int32, (BLOCK, BLOCK), 1)
        tri = (row <= col).astype(jnp.float32)
        carry = jnp.zeros((N_HEADS, 1), jnp.float32)
        for ch in range(nb):
            cs = jnp.dot(logf[:, ch * BLOCK:(ch + 1) * BLOCK], tri,
                         precision=lax.Precision.HIGHEST,
                         preferred_element_type=jnp.float32) + carry
            c_sc[:, ch * BLOCK:(ch + 1) * BLOCK] = cs
            carry = cs[:, BLOCK - 1:BLOCK]

    k = k_ref[0]
    v = v_ref[0]
    causal = _causal_tile()
    crows = [c_sc[pl.ds(HEADS_PER_STEP * p + hh, 1), :] for hh in range(HEADS_PER_STEP)]
    for i in range(nb):
        n = (i + 1) * BLOCK
        q = q_ref[0, i * BLOCK:(i + 1) * BLOCK, :]
        outs = []
        for hh in range(HEADS_PER_STEP):
            hmask = _head_mask(hh)
            qm = jnp.where(hmask, q, jnp.zeros_like(q))
            s = lax.dot_general(qm, k[:n], _NT, preferred_element_type=jnp.float32)
            crow = crows[hh]
            s = s * SCALE + (crow[:, i * BLOCK:i * BLOCK + 1] - crow[:, :n])
            diag = jnp.where(causal, s[:, i * BLOCK:], NEG)
            s = jnp.concatenate([s[:, :i * BLOCK], diag], axis=1) if i > 0 else diag
            outs.append((hmask, _softmax_pv(s, v[:n])))
        o = jnp.where(outs[0][0], outs[0][1], outs[1][1])
        o_ref[0, i * BLOCK:(i + 1) * BLOCK, :] = o.astype(o_ref.dtype)


def _fox(qkv3, flog, bf):
    b, seq, _ = qkv3.shape
    base = 3 * N_PAIRS
    blk = lambda off: pl.BlockSpec((1, seq, LANES), lambda bi, p, off=off: (bi, 0, off + p))
    return pl.pallas_call(
        functools.partial(_fox_kernel, seq=seq),
        grid=(b, N_PAIRS),
        in_specs=[
            blk(base), blk(base + N_PAIRS), blk(base + 2 * N_PAIRS),
            pl.BlockSpec((N_HEADS, seq), lambda bi, p: (0, bi)),
            pl.BlockSpec((N_HEADS, 1), lambda bi, p: (0, 0)),
        ],
        out_specs=pl.BlockSpec((1, seq, LANES), lambda bi, p: (bi, 0, p)),
        out_shape=jax.ShapeDtypeStruct((b, seq, GROUP_WIDTH), jnp.bfloat16),
        scratch_shapes=[pltpu.VMEM((N_HEADS, seq), jnp.float32)],
        compiler_params=pltpu.CompilerParams(
            dimension_semantics=("arbitrary", "arbitrary"), vmem_limit_bytes=VMEM_LIMIT),
        name="fox_attn",
    )(qkv3, qkv3, qkv3, flog, bf)


def _dense_kernel(x_ref, ym_ref, yf_ref, wo_ref, g_ref, wg_ref, wu_ref, wd_ref, gfin_ref, o_ref,
                  *, final):
    x1 = (x_ref[...]
          + jnp.dot(ym_ref[...], wo_ref[:GROUP_WIDTH, :], preferred_element_type=jnp.float32)
          + jnp.dot(yf_ref[...], wo_ref[GROUP_WIDTH:, :], preferred_element_type=jnp.float32))
    h = _rms(x1, g_ref[...]).astype(jnp.bfloat16)
    acc = jnp.zeros_like(x1)
    for c in range(D_FF // FF_CHUNK):
        gate = jnp.dot(h, wg_ref[c], preferred_element_type=jnp.float32)
        up = jnp.dot(h, wu_ref[c], preferred_element_type=jnp.float32)
        a = (gate / (1.0 + jnp.exp(-gate)) * up).astype(jnp.bfloat16)
        acc = acc + jnp.dot(a, wd_ref[c], preferred_element_type=jnp.float32)
    acc = acc + x1
    if final:
        acc = _rms(acc, gfin_ref[...])
    o_ref[...] = acc


def _dense(x2d, ym, yf, wo, g, wg, wu, wd, gfin, final):
    t = x2d.shape[0]
    nc = D_FF // FF_CHUNK
    const = lambda shape: pl.BlockSpec(shape, lambda i: (0,) * len(shape),
                                       pipeline_mode=pl.Buffered(1))
    return pl.pallas_call(
        functools.partial(_dense_kernel, final=final),
        grid=(t // TOKEN_TILE,),
        in_specs=[
            pl.BlockSpec((TOKEN_TILE, D_MODEL), lambda i: (i, 0)),
            pl.BlockSpec((TOKEN_TILE, GROUP_WIDTH), lambda i: (i, 0)),
            pl.BlockSpec((TOKEN_TILE, GROUP_WIDTH), lambda i: (i, 0)),
            const((D_MODEL, D_MODEL)),
            const((1, D_MODEL)),
            const((nc, D_MODEL, FF_CHUNK)),
            const((nc, D_MODEL, FF_CHUNK)),
            const((nc, FF_CHUNK, D_MODEL)),
            const((1, D_MODEL)),
        ],
        out_specs=pl.BlockSpec((TOKEN_TILE, D_MODEL), lambda i: (i, 0)),
        out_shape=jax.ShapeDtypeStruct((t, D_MODEL), jnp.float32),
        compiler_params=pltpu.CompilerParams(
            dimension_semantics=("arbitrary",), vmem_limit_bytes=VMEM_LIMIT),
        name="dense_ffn",
    )(x2d, ym, yf, wo, g, wg, wu, wd, gfin)


def kernel(x, w_in, b_f, w_o, g_attn, w_gu, w_down, g_ffn, rel_bias, g_final):
    b, seq, d = x.shape
    depth = w_in.shape[0]
    assert d == D_MODEL and seq % BLOCK == 0 and (b * seq) % TOKEN_TILE == 0
    bf16 = jnp.bfloat16
    nc = D_FF // FF_CHUNK
    bias = _bias_strips(rel_bias, seq)
    x2d = x.reshape(b * seq, d)
    for layer in range(depth):
        w_qkv = w_in[layer, :, :QKV_WIDTH].astype(bf16)
        w_f = jnp.pad(w_in[layer, :, QKV_WIDTH:].T, ((0, N_HEADS), (0, 0))).astype(bf16)
        w_g = w_gu[layer, :, :D_FF].reshape(d, nc, FF_CHUNK).transpose(1, 0, 2).astype(bf16)
        w_u = w_gu[layer, :, D_FF:].reshape(d, nc, FF_CHUNK).transpose(1, 0, 2).astype(bf16)
        w_d = w_down[layer].reshape(nc, FF_CHUNK, d).astype(bf16)
        qkv, flog = _inproj(x2d, g_attn[layer][None, :], w_qkv, w_f)
        qkv3 = qkv.reshape(b, seq, QKV_WIDTH)
        y_m = _moba(qkv3, bias)
        y_f = _fox(qkv3, flog, b_f[layer][:, None])
        x2d = _dense(x2d, y_m.reshape(b * seq, GROUP_WIDTH), y_f.reshape(b * seq, GROUP_WIDTH),
                     w_o[layer].astype(bf16), g_ffn[layer][None, :], w_g, w_u, w_d,
                     g_final[None, :], final=(layer == depth - 1))
    return x2d.reshape(b, seq, d)
```

```python
import functools
import math

import jax
import jax.numpy as jnp
from jax import lax
from jax.experimental import pallas as pl
from jax.experimental.pallas import tpu as pltpu

D_MODEL = 1024
HEAD_DIM = 64
N_HEADS = 8
GROUP_WIDTH = N_HEADS * HEAD_DIM
QKV_WIDTH = 6 * GROUP_WIDTH
BLOCK = 256
TOPK = 3
NUM_BUCKETS = 32
MAX_DISTANCE = 1024
D_FF = 2816
RMS_EPS = 1e-6
SCALE = HEAD_DIM ** -0.5
NEG = -1e30

LANES = 128
HEADS_PER_STEP = LANES // HEAD_DIM
N_PAIRS = N_HEADS // HEADS_PER_STEP
FF_CHUNK = 256
TOKEN_TILE = 512
VMEM_LIMIT = 56 * 1024 * 1024

_NT = (((1,), (1,)), ((), ()))


def _rms(x, g):
    return x * lax.rsqrt(jnp.mean(x * x, axis=-1, keepdims=True) + RMS_EPS) * g


def _inproj_kernel(x_ref, g_ref, w_ref, wf_ref, qkv_ref, flog_ref):
    h = _rms(x_ref[...], g_ref[...]).astype(jnp.bfloat16)
    qkv_ref[...] = jnp.dot(h, w_ref[...], preferred_element_type=jnp.float32).astype(jnp.bfloat16)
    fl = lax.dot_general(wf_ref[...], h, _NT, preferred_element_type=jnp.float32)
    flog_ref[...] = fl[:N_HEADS]


def _inproj(x2d, g, w, wf):
    t = x2d.shape[0]
    return pl.pallas_call(
        _inproj_kernel,
        grid=(t // TOKEN_TILE,),
        in_specs=[
            pl.BlockSpec((TOKEN_TILE, D_MODEL), lambda i: (i, 0)),
            pl.BlockSpec((1, D_MODEL), lambda i: (0, 0)),
            pl.BlockSpec((D_MODEL, QKV_WIDTH), lambda i: (0, 0)),
            pl.BlockSpec((2 * N_HEADS, D_MODEL), lambda i: (0, 0)),
        ],
        out_specs=[
            pl.BlockSpec((TOKEN_TILE, QKV_WIDTH), lambda i: (i, 0)),
            pl.BlockSpec((N_HEADS, TOKEN_TILE), lambda i: (0, i)),
        ],
        out_shape=[
            jax.ShapeDtypeStruct((t, QKV_WIDTH), jnp.bfloat16),
            jax.ShapeDtypeStruct((N_HEADS, t), jnp.float32),
        ],
        compiler_params=pltpu.CompilerParams(
            dimension_semantics=("arbitrary",), vmem_limit_bytes=VMEM_LIMIT),
        name="inproj",
    )(x2d, g, w, wf)


def _bias_kernel(relt_ref, bkt_ref, out_ref, *, seq):
    h = pl.program_id(0)
    bkt = bkt_ref[...]
    w = jnp.zeros(bkt.shape, jnp.float32)
    for b in range(NUM_BUCKETS):
        w = jnp.where(bkt == b, relt_ref[h, b], w)
    x = jnp.broadcast_to(w, (BLOCK, bkt.shape[1]))
    y = pltpu.roll(x, 0, 1, stride=1, stride_axis=0)
    out_ref[0] = y[:, BLOCK:]


def _t5_bucket(dist):
    max_exact = NUM_BUCKETS // 2
    is_small = dist < max_exact
    d = jnp.maximum(dist, 1).astype(jnp.float32)
    large = max_exact + (jnp.log(d / max_exact) / math.log(MAX_DISTANCE / max_exact)
                         * (NUM_BUCKETS - max_exact)).astype(jnp.int32)
    large = jnp.minimum(large, NUM_BUCKETS - 1)
    return jnp.where(is_small, dist, large)


def _bias_strips(rel_bias, seq):
    n = jnp.arange(seq + BLOCK, dtype=jnp.int32)
    bkt = _t5_bucket(jnp.clip(seq - n, 0, seq - 1))[None, :]
    return pl.pallas_call(
        functools.partial(_bias_kernel, seq=seq),
        grid=(N_HEADS,),
        in_specs=[
            pl.BlockSpec(memory_space=pltpu.SMEM),
            pl.BlockSpec((1, seq + BLOCK), lambda h: (0, 0)),
        ],
        out_specs=pl.BlockSpec((1, BLOCK, seq), lambda h: (h, 0, 0)),
        out_shape=jax.ShapeDtypeStruct((N_HEADS, BLOCK, seq), jnp.float32),
        compiler_params=pltpu.CompilerParams(dimension_semantics=("arbitrary",)),
        name="moba_bias",
    )(rel_bias.T, bkt)


def _head_mask(hh):
    lane = lax.broadcasted_iota(jnp.int32, (1, LANES), 1)
    return (lane >= hh * HEAD_DIM) & (lane < (hh + 1) * HEAD_DIM)


def _causal_tile():
    row = lax.broadcasted_iota(jnp.int32, (BLOCK, BLOCK), 0)
    col = lax.broadcasted_iota(jnp.int32, (BLOCK, BLOCK), 1)
    return row >= col


def _fold_lanes(x, op):
    out = x[:, :LANES]
    for g in range(1, BLOCK // LANES):
        out = op(out, x[:, g * LANES:(g + 1) * LANES])
    return out


class _Unit:
    def __init__(self, i, hh, qm, tile_term, s_ref):
        self.i, self.hh, self.qm, self.tile_term, self.s_ref = i, hh, qm, tile_term, s_ref
        self.mxp = self.m = self.lp = self.acc = None


def _score_tile(u, j, k_ref, causal):
    s = lax.dot_general(u.qm, k_ref[0, j * BLOCK:(j + 1) * BLOCK, :], _NT,
                        preferred_element_type=jnp.float32)
    s = s + u.tile_term(j)
    if j == u.i:
        s = jnp.where(causal, s, NEG)
    u.s_ref[:, j * BLOCK:(j + 1) * BLOCK] = s
    t = _fold_lanes(s, jnp.maximum)
    u.mxp = t if u.mxp is None else jnp.maximum(u.mxp, t)


def _value_tile(u, j, v_ref):
    if u.m is None:
        u.m = jnp.max(u.mxp, axis=-1, keepdims=True)
    e = jnp.exp(u.s_ref[:, j * BLOCK:(j + 1) * BLOCK] - u.m)
    t = _fold_lanes(e, jnp.add)
    u.lp = t if u.lp is None else u.lp + t
    pv = jnp.dot(e.astype(jnp.bfloat16), v_ref[0, j * BLOCK:(j + 1) * BLOCK, :],
                 preferred_element_type=jnp.float32)
    u.acc = pv if u.acc is None else u.acc + pv


def _run_units(nb, make_unit, k_ref, v_ref, o_ref):
    causal = _causal_tile()
    order = [(i, hh) for i in range(nb) for hh in range(HEADS_PER_STEP)]
    prev = None
    for nxt in order + [None]:
        cur = make_unit(*nxt) if nxt is not None else None
        n1 = cur.i + 1 if cur is not None else 0
        n2 = prev.i + 1 if prev is not None else 0
        for t in range(max(n1, n2)):
            if t < n1:
                _score_tile(cur, t, k_ref, causal)
            if t < n2:
                _value_tile(prev, t, v_ref)
        if prev is not None:
            out = prev.acc * (1.0 / jnp.sum(prev.lp, axis=-1, keepdims=True))
            lo = prev.hh * HEAD_DIM
            o_ref[0, prev.i * BLOCK:(prev.i + 1) * BLOCK, lo:lo + HEAD_DIM] = (
                out[:, lo:lo + HEAD_DIM].astype(o_ref.dtype))
        prev = cur


def _masked_scaled_q(q, hmask):
    return jnp.where(hmask, q, jnp.zeros_like(q)) * jnp.asarray(SCALE, q.dtype)


def _moba_kernel(q_ref, k_ref, v_ref, bias_ref, o_ref, s_sc, *, seq):
    nb = seq // BLOCK
    kmean = jnp.sum(k_ref[0].astype(jnp.float32).reshape(nb, BLOCK, LANES), axis=1) * (1.0 / BLOCK)
    kmean = jnp.concatenate([kmean, jnp.zeros((LANES - nb, LANES), jnp.float32)], axis=0)
    blk_lane = lax.broadcasted_iota(jnp.int32, (BLOCK, LANES), 1)

    def make_unit(i, hh):
        qm = _masked_scaled_q(q_ref[0, i * BLOCK:(i + 1) * BLOCK, :], _head_mask(hh))
        pen = None
        if i > TOPK:
            gate = lax.dot_general(qm.astype(jnp.float32), kmean, _NT,
                                   precision=lax.Precision.HIGHEST,
                                   preferred_element_type=jnp.float32)
            rank = jnp.zeros((BLOCK, LANES), jnp.int32)
            for jp in range(i):
                col = gate[:, jp:jp + 1]
                beats = (col > gate) | ((col == gate) & (blk_lane > jp))
                rank = rank + beats.astype(jnp.int32)
            pen = jnp.where(rank < TOPK, 0.0, NEG)

        def tile_term(j):
            off = (nb - 1 - i + j) * BLOCK
            term = bias_ref[hh, :, off:off + BLOCK]
            if pen is not None and j < i:
                term = term + pen[:, j:j + 1]
            return term

        return _Unit(i, hh, qm, tile_term, s_sc.at[hh])

    _run_units(nb, make_unit, k_ref, v_ref, o_ref)


def _moba(qkv3, bias):
    b, seq, _ = qkv3.shape
    blk = lambda off: pl.BlockSpec((1, seq, LANES), lambda p, bi, off=off: (bi, 0, off + p))
    return pl.pallas_call(
        functools.partial(_moba_kernel, seq=seq),
        grid=(N_PAIRS, b),
        in_specs=[
            blk(0), blk(N_PAIRS), blk(2 * N_PAIRS),
            pl.BlockSpec((HEADS_PER_STEP, BLOCK, seq), lambda p, bi: (p, 0, 0)),
        ],
        out_specs=pl.BlockSpec((1, seq, LANES), lambda p, bi: (bi, 0, p)),
        out_shape=jax.ShapeDtypeStruct((b, seq, GROUP_WIDTH), jnp.bfloat16),
        scratch_shapes=[pltpu.VMEM((HEADS_PER_STEP, BLOCK, seq), jnp.float32)],
        compiler_params=pltpu.CompilerParams(
            dimension_semantics=("arbitrary", "arbitrary"), vmem_limit_bytes=VMEM_LIMIT),
        name="moba_attn",
    )(qkv3, qkv3, qkv3, bias)


def _fox_kernel(q_ref, k_ref, v_ref, flog_ref, bf_ref, o_ref, c_sc, s_sc, *, seq):
    nb = seq // BLOCK
    p = pl.program_id(1)

    @pl.when(p == 0)
    def _():
        z = flog_ref[...] + bf_ref[...]
        logf = -(jnp.maximum(-z, 0.0) + jnp.log1p(jnp.exp(-jnp.abs(z))))
        row = lax.broadcasted_iota(jnp.int32, (BLOCK, BLOCK), 0)
        col = lax.broadcasted_iota(jnp.int32, (BLOCK, BLOCK), 1)
        tri = (row <= col).astype(jnp.float32)
        carry = jnp.zeros((N_HEADS, 1), jnp.float32)
        for ch in range(nb):
            cs = jnp.dot(logf[:, ch * BLOCK:(ch + 1) * BLOCK], tri,
                         precision=lax.Precision.HIGHEST,
                         preferred_element_type=jnp.float32) + carry
            c_sc[:, ch * BLOCK:(ch + 1) * BLOCK] = cs
            carry = cs[:, BLOCK - 1:BLOCK]

    def make_unit(i, hh):
        qm = _masked_scaled_q(q_ref[0, i * BLOCK:(i + 1) * BLOCK, :], _head_mask(hh))
        row = HEADS_PER_STEP * p + hh
        c_start = c_sc[pl.ds(row, 1), i * BLOCK:(i + 1) * BLOCK][:, 0:1]

        def tile_term(j):
            return c_start - c_sc[pl.ds(row, 1), j * BLOCK:(j + 1) * BLOCK]

        return _Unit(i, hh, qm, tile_term, s_sc.at[hh])

    _run_units(nb, make_unit, k_ref, v_ref, o_ref)


def _fox(qkv3, flog, bf):
    b, seq, _ = qkv3.shape
    base = 3 * N_PAIRS
    blk = lambda off: pl.BlockSpec((1, seq, LANES), lambda bi, p, off=off: (bi, 0, off + p))
    return pl.pallas_call(
        functools.partial(_fox_kernel, seq=seq),
        grid=(b, N_PAIRS),
        in_specs=[
            blk(base), blk(base + N_PAIRS), blk(base + 2 * N_PAIRS),
            pl.BlockSpec((N_HEADS, seq), lambda bi, p: (0, bi)),
            pl.BlockSpec((N_HEADS, 1), lambda bi, p: (0, 0)),
        ],
        out_specs=pl.BlockSpec((1, seq, LANES), lambda bi, p: (bi, 0, p)),
        out_shape=jax.ShapeDtypeStruct((b, seq, GROUP_WIDTH), jnp.bfloat16),
        scratch_shapes=[pltpu.VMEM((N_HEADS, seq), jnp.float32),
                        pltpu.VMEM((HEADS_PER_STEP, BLOCK, seq), jnp.float32)],
        compiler_params=pltpu.CompilerParams(
            dimension_semantics=("arbitrary", "arbitrary"), vmem_limit_bytes=VMEM_LIMIT),
        name="fox_attn",
    )(qkv3, qkv3, qkv3, flog, bf)


def _dense_kernel(x_ref, ym_ref, yf_ref, wo_ref, g_ref, wg_ref, wu_ref, wd_ref, gfin_ref, o_ref,
                  *, final):
    x1 = (x_ref[...]
          + jnp.dot(ym_ref[...], wo_ref[:GROUP_WIDTH, :], preferred_element_type=jnp.float32)
          + jnp.dot(yf_ref[...], wo_ref[GROUP_WIDTH:, :], preferred_element_type=jnp.float32))
    h = _rms(x1, g_ref[...]).astype(jnp.bfloat16)
    acc = jnp.zeros_like(x1)
    for c in range(D_FF // FF_CHUNK):
        gate = jnp.dot(h, wg_ref[c], preferred_element_type=jnp.float32)
        up = jnp.dot(h, wu_ref[c], preferred_element_type=jnp.float32)
        a = (gate / (1.0 + jnp.exp(-gate)) * up).astype(jnp.bfloat16)
        acc = acc + jnp.dot(a, wd_ref[c], preferred_element_type=jnp.float32)
    acc = acc + x1
    if final:
        acc = _rms(acc, gfin_ref[...])
    o_ref[...] = acc


def _dense(x2d, ym, yf, wo, g, wg, wu, wd, gfin, final):
    t = x2d.shape[0]
    nc = D_FF // FF_CHUNK
    const = lambda shape: pl.BlockSpec(shape, lambda i: (0,) * len(shape),
                                       pipeline_mode=pl.Buffered(1))
    return pl.pallas_call(
        functools.partial(_dense_kernel, final=final),
        grid=(t // TOKEN_TILE,),
        in_specs=[
            pl.BlockSpec((TOKEN_TILE, D_MODEL), lambda i: (i, 0)),
            pl.BlockSpec((TOKEN_TILE, GROUP_WIDTH), lambda i: (i, 0)),
            pl.BlockSpec((TOKEN_TILE, GROUP_WIDTH), lambda i: (i, 0)),
            const((D_MODEL, D_MODEL)),
            const((1, D_MODEL)),
            const((nc, D_MODEL, FF_CHUNK)),
            const((nc, D_MODEL, FF_CHUNK)),
            const((nc, FF_CHUNK, D_MODEL)),
            const((1, D_MODEL)),
        ],
        out_specs=pl.BlockSpec((TOKEN_TILE, D_MODEL), lambda i: (i, 0)),
        out_shape=jax.ShapeDtypeStruct((t, D_MODEL), jnp.float32),
        compiler_params=pltpu.CompilerParams(
            dimension_semantics=("arbitrary",), vmem_limit_bytes=VMEM_LIMIT),
        name="dense_ffn",
    )(x2d, ym, yf, wo, g, wg, wu, wd, gfin)


def kernel(x, w_in, b_f, w_o, g_attn, w_gu, w_down, g_ffn, rel_bias, g_final):
    b, seq, d = x.shape
    depth = w_in.shape[0]
    assert d == D_MODEL and seq % BLOCK == 0 and (b * seq) % TOKEN_TILE == 0
    bf16 = jnp.bfloat16
    nc = D_FF // FF_CHUNK
    bias = _bias_strips(rel_bias, seq)
    x2d = x.reshape(b * seq, d)
    for layer in range(depth):
        w_qkv = w_in[layer, :, :QKV_WIDTH].astype(bf16)
        w_f = jnp.pad(w_in[layer, :, QKV_WIDTH:].T, ((0, N_HEADS), (0, 0))).astype(bf16)
        w_g = w_gu[layer, :, :D_FF].reshape(d, nc, FF_CHUNK).transpose(1, 0, 2).astype(bf16)
        w_u = w_gu[layer, :, D_FF:].reshape(d, nc, FF_CHUNK).transpose(1, 0, 2).astype(bf16)
        w_d = w_down[layer].reshape(nc, FF_CHUNK, d).astype(bf16)
        qkv, flog = _inproj(x2d, g_attn[layer][None, :], w_qkv, w_f)
        qkv3 = qkv.reshape(b, seq, QKV_WIDTH)
        y_m = _moba(qkv3, bias)
        y_f = _fox(qkv3, flog, b_f[layer][:, None])
        x2d = _dense(x2d, y_m.reshape(b * seq, GROUP_WIDTH), y_f.reshape(b * seq, GROUP_WIDTH),
                     w_o[layer].astype(bf16), g_ffn[layer][None, :], w_g, w_u, w_d,
                     g_final[None, :], final=(layer == depth - 1))
    return x2d.reshape(b, seq, d)
```

```python
import functools
import math

import jax
import jax.numpy as jnp
from jax import lax
from jax.experimental import pallas as pl
from jax.experimental.pallas import tpu as pltpu

D_MODEL = 1024
HEAD_DIM = 64
N_HEADS = 8
GROUP_WIDTH = N_HEADS * HEAD_DIM
QKV_WIDTH = 6 * GROUP_WIDTH
BLOCK = 256
TOPK = 3
NUM_BUCKETS = 32
MAX_DISTANCE = 1024
D_FF = 2816
RMS_EPS = 1e-6
SCALE = HEAD_DIM ** -0.5
NEG = -1e30

LANES = 128
HEADS_PER_STEP = LANES // HEAD_DIM
N_PAIRS = N_HEADS // HEADS_PER_STEP
C_PARTS = 3
FF_CHUNK = 256
TOKEN_TILE = 512
VMEM_LIMIT = 56 * 1024 * 1024

_NT = (((1,), (1,)), ((), ()))


def _rms(x, g):
    return x * lax.rsqrt(jnp.mean(x * x, axis=-1, keepdims=True) + RMS_EPS) * g


def _inproj_kernel(x_ref, g_ref, w_ref, wf_ref, bf_ref, qkv_ref, logf_ref):
    h = _rms(x_ref[...], g_ref[...]).astype(jnp.bfloat16)
    qkv_ref[...] = jnp.dot(h, w_ref[...], preferred_element_type=jnp.float32).astype(jnp.bfloat16)
    z = jnp.dot(h, wf_ref[...], preferred_element_type=jnp.float32) + bf_ref[...]
    logf_ref[...] = -(jnp.maximum(-z, 0.0) + jnp.log1p(jnp.exp(-jnp.abs(z))))


def _inproj(x2d, g, w, wf, bf):
    t = x2d.shape[0]
    return pl.pallas_call(
        _inproj_kernel,
        grid=(t // TOKEN_TILE,),
        in_specs=[
            pl.BlockSpec((TOKEN_TILE, D_MODEL), lambda i: (i, 0)),
            pl.BlockSpec((1, D_MODEL), lambda i: (0, 0)),
            pl.BlockSpec((D_MODEL, QKV_WIDTH), lambda i: (0, 0)),
            pl.BlockSpec((D_MODEL, LANES), lambda i: (0, 0)),
            pl.BlockSpec((1, LANES), lambda i: (0, 0)),
        ],
        out_specs=[
            pl.BlockSpec((TOKEN_TILE, QKV_WIDTH), lambda i: (i, 0)),
            pl.BlockSpec((TOKEN_TILE, LANES), lambda i: (i, 0)),
        ],
        out_shape=[
            jax.ShapeDtypeStruct((t, QKV_WIDTH), jnp.bfloat16),
            jax.ShapeDtypeStruct((t, LANES), jnp.float32),
        ],
        compiler_params=pltpu.CompilerParams(
            dimension_semantics=("arbitrary",), vmem_limit_bytes=VMEM_LIMIT),
        name="inproj",
    )(x2d, g, w, wf, bf)


def _forget_lane_layout(cols):
    rep = jnp.repeat(cols, C_PARTS, axis=-1)
    half = jnp.pad(rep, [(0, 0)] * (cols.ndim - 1) + [(0, HEAD_DIM - C_PARTS * N_HEADS)])
    return jnp.concatenate([half, half], axis=-1)


def _bias_kernel(relt_ref, bkt_ref, out_ref, *, seq):
    h = pl.program_id(0)
    bkt = bkt_ref[...]
    w = jnp.zeros(bkt.shape, jnp.float32)
    for b in range(NUM_BUCKETS):
        w = jnp.where(bkt == b, relt_ref[h, b], w)
    x = jnp.broadcast_to(w, (BLOCK, bkt.shape[1]))
    y = pltpu.roll(x, 0, 1, stride=1, stride_axis=0)
    out_ref[0] = y[:, BLOCK:]


def _t5_bucket(dist):
    max_exact = NUM_BUCKETS // 2
    is_small = dist < max_exact
    d = jnp.maximum(dist, 1).astype(jnp.float32)
    large = max_exact + (jnp.log(d / max_exact) / math.log(MAX_DISTANCE / max_exact)
                         * (NUM_BUCKETS - max_exact)).astype(jnp.int32)
    large = jnp.minimum(large, NUM_BUCKETS - 1)
    return jnp.where(is_small, dist, large)


def _bias_strips(rel_bias, seq):
    n = jnp.arange(seq + BLOCK, dtype=jnp.int32)
    bkt = _t5_bucket(jnp.clip(seq - n, 0, seq - 1))[None, :]
    return pl.pallas_call(
        functools.partial(_bias_kernel, seq=seq),
        grid=(N_HEADS,),
        in_specs=[
            pl.BlockSpec(memory_space=pltpu.SMEM),
            pl.BlockSpec((1, seq + BLOCK), lambda h: (0, 0)),
        ],
        out_specs=pl.BlockSpec((1, BLOCK, seq), lambda h: (h, 0, 0)),
        out_shape=jax.ShapeDtypeStruct((N_HEADS, BLOCK, seq), jnp.float32),
        compiler_params=pltpu.CompilerParams(dimension_semantics=("arbitrary",)),
        name="moba_bias",
    )(rel_bias.T, bkt)


def _lane_iota(rows):
    return lax.broadcasted_iota(jnp.int32, (rows, LANES), 1)


def _head_mask(hh, rows=1):
    lane = _lane_iota(rows)
    return (lane >= hh * HEAD_DIM) & (lane < (hh + 1) * HEAD_DIM)


def _spare_base(hh):
    return (1 - hh) * HEAD_DIM


def _causal_tile():
    row = lax.broadcasted_iota(jnp.int32, (BLOCK, BLOCK), 0)
    col = lax.broadcasted_iota(jnp.int32, (BLOCK, BLOCK), 1)
    return row >= col


def _fold_lanes(x, op):
    out = x[:, :LANES]
    for g in range(1, BLOCK // LANES):
        out = op(out, x[:, g * LANES:(g + 1) * LANES])
    return out


def _augment_kv(k_ref, v_ref, k_extra, kp_sc, vp_sc):
    seq = k_ref.shape[1]
    for hh in range(HEADS_PER_STEP):
        hmask = _head_mask(hh, seq)
        kp_sc[hh] = jnp.where(hmask, k_ref[0], k_extra)
        vp_sc[hh] = jnp.where(hmask, v_ref[0], jnp.ones((seq, LANES), v_ref.dtype))


class _Unit:
    def __init__(self, i, hh, qa, tile_term, s_ref):
        self.i, self.hh, self.qa, self.tile_term, self.s_ref = i, hh, qa, tile_term, s_ref
        self.mxp = self.m = self.acc = None


def _score_tile(u, j, kp_sc, causal):
    s = lax.dot_general(u.qa, kp_sc[u.hh, j * BLOCK:(j + 1) * BLOCK, :], _NT,
                        preferred_element_type=jnp.float32)
    if u.tile_term is not None:
        s = s + u.tile_term(j)
    if j == u.i:
        s = jnp.where(causal, s, NEG)
    u.s_ref[:, j * BLOCK:(j + 1) * BLOCK] = s
    t = _fold_lanes(s, jnp.maximum)
    u.mxp = t if u.mxp is None else jnp.maximum(u.mxp, t)


def _value_tile(u, j, vp_sc):
    if u.m is None:
        u.m = jnp.max(u.mxp, axis=-1, keepdims=True)
    e = jnp.exp(u.s_ref[:, j * BLOCK:(j + 1) * BLOCK] - u.m)
    pv = jnp.dot(e.astype(jnp.bfloat16), vp_sc[u.hh, j * BLOCK:(j + 1) * BLOCK, :],
                 preferred_element_type=jnp.float32)
    u.acc = pv if u.acc is None else u.acc + pv


def _run_units(nb, make_unit, kp_sc, vp_sc, o_ref):
    causal = _causal_tile()
    order = [(i, hh) for i in range(nb) for hh in range(HEADS_PER_STEP)]
    prev = None
    for nxt in order + [None]:
        cur = make_unit(*nxt) if nxt is not None else None
        n1 = cur.i + 1 if cur is not None else 0
        n2 = prev.i + 1 if prev is not None else 0
        for t in range(max(n1, n2)):
            if t < n1:
                _score_tile(cur, t, kp_sc, causal)
            if t < n2:
                _value_tile(prev, t, vp_sc)
        if prev is not None:
            out = prev.acc * (1.0 / pltpu.roll(prev.acc, HEAD_DIM, 1))
            lo = prev.hh * HEAD_DIM
            o_ref[0, prev.i * BLOCK:(prev.i + 1) * BLOCK, lo:lo + HEAD_DIM] = (
                out[:, lo:lo + HEAD_DIM].astype(o_ref.dtype))
        prev = cur


def _scaled_q(q):
    return q * jnp.asarray(SCALE, q.dtype)


def _moba_kernel(q_ref, k_ref, v_ref, bias_ref, o_ref, s_sc, kp_sc, vp_sc, *, seq):
    nb = seq // BLOCK
    key_blk = lax.broadcasted_iota(jnp.int32, (seq, LANES), 0) // BLOCK
    onehot = ((_lane_iota(seq) % HEAD_DIM) == key_blk).astype(k_ref.dtype)
    _augment_kv(k_ref, v_ref, onehot, kp_sc, vp_sc)
    kmean = jnp.sum(k_ref[0].astype(jnp.float32).reshape(nb, BLOCK, LANES), axis=1) * (1.0 / BLOCK)
    lane = _lane_iota(BLOCK)

    def make_unit(i, hh):
        hmask = _head_mask(hh, BLOCK)
        sp = _spare_base(hh)
        q = jnp.where(hmask, _scaled_q(q_ref[0, i * BLOCK:(i + 1) * BLOCK, :]), 0)
        pen = jnp.zeros((BLOCK, LANES), q.dtype)
        if i > TOPK:
            pads = [jnp.zeros((n, LANES), jnp.float32) for n in (sp, LANES - sp - nb)]
            kmean_rows = jnp.concatenate([a for a in (pads[0], kmean, pads[1]) if a.shape[0]], axis=0)
            gate = lax.dot_general(q.astype(jnp.float32), kmean_rows, _NT,
                                   precision=lax.Precision.HIGHEST,
                                   preferred_element_type=jnp.float32)
            rank = jnp.zeros((BLOCK, LANES), jnp.int32)
            for jp in range(i):
                col = gate[:, sp + jp:sp + jp + 1]
                beats = (col > gate) | ((col == gate) & (lane > sp + jp))
                rank = rank + beats.astype(jnp.int32)
            past = (lane >= sp) & (lane < sp + i)
            pen = jnp.where(past & (rank >= TOPK), NEG, 0.0).astype(q.dtype)
        qa = jnp.where(hmask, q, pen)

        def tile_term(j):
            off = (nb - 1 - i + j) * BLOCK
            return bias_ref[hh, :, off:off + BLOCK]

        return _Unit(i, hh, qa, tile_term, s_sc.at[hh])

    _run_units(nb, make_unit, kp_sc, vp_sc, o_ref)


def _attn_scratch(seq):
    return [pltpu.VMEM((HEADS_PER_STEP, BLOCK, seq), jnp.float32),
            pltpu.VMEM((HEADS_PER_STEP, seq, LANES), jnp.bfloat16),
            pltpu.VMEM((HEADS_PER_STEP, seq, LANES), jnp.bfloat16)]


def _moba(qkv3, bias):
    b, seq, _ = qkv3.shape
    blk = lambda off: pl.BlockSpec((1, seq, LANES), lambda p, bi, off=off: (bi, 0, off + p))
    return pl.pallas_call(
        functools.partial(_moba_kernel, seq=seq),
        grid=(N_PAIRS, b),
        in_specs=[
            blk(0), blk(N_PAIRS), blk(2 * N_PAIRS),
            pl.BlockSpec((HEADS_PER_STEP, BLOCK, seq), lambda p, bi: (p, 0, 0)),
        ],
        out_specs=pl.BlockSpec((1, seq, LANES), lambda p, bi: (bi, 0, p)),
        out_shape=jax.ShapeDtypeStruct((b, seq, GROUP_WIDTH), jnp.bfloat16),
        scratch_shapes=_attn_scratch(seq),
        compiler_params=pltpu.CompilerParams(
            dimension_semantics=("arbitrary", "arbitrary"), vmem_limit_bytes=VMEM_LIMIT),
        name="moba_attn",
    )(qkv3, qkv3, qkv3, bias)


def _split3(x):
    p1 = x.astype(jnp.bfloat16)
    r1 = x - p1.astype(jnp.float32)
    p2 = r1.astype(jnp.bfloat16)
    p3 = (r1 - p2.astype(jnp.float32)).astype(jnp.bfloat16)
    return p1, p2, p3


def _fox_kernel(q_ref, k_ref, v_ref, logf_ref, o_ref, ext_sc, s_sc, kp_sc, vp_sc, *, seq):
    nb = seq // BLOCK
    p = pl.program_id(1)

    @pl.when(p == 0)
    def _():
        row = lax.broadcasted_iota(jnp.int32, (BLOCK, BLOCK), 0)
        col = lax.broadcasted_iota(jnp.int32, (BLOCK, BLOCK), 1)
        tri = (col <= row).astype(jnp.bfloat16)
        lane = _lane_iota(BLOCK) % HEAD_DIM
        carry = jnp.zeros((1, LANES), jnp.float32)
        for ch in range(nb):
            parts = _split3(logf_ref[ch * BLOCK:(ch + 1) * BLOCK, :])
            cs = jnp.dot(tri, jnp.concatenate(parts, axis=1), preferred_element_type=jnp.float32)
            c = cs[:, :LANES] + cs[:, LANES:2 * LANES] + cs[:, 2 * LANES:] + carry
            carry = c[BLOCK - 1:BLOCK, :]
            p1, p2, p3 = _split3(-c)
            part = lane % C_PARTS
            ext = jnp.where(part == 0, p1, jnp.where(part == 1, p2, p3))
            ext_sc[ch * BLOCK:(ch + 1) * BLOCK, :] = jnp.where(
                lane < C_PARTS * N_HEADS, ext, jnp.zeros_like(ext))

    _augment_kv(k_ref, v_ref, ext_sc[...], kp_sc, vp_sc)
    lane = _lane_iota(BLOCK)

    def make_unit(i, hh):
        first = _spare_base(hh) + C_PARTS * (HEADS_PER_STEP * p + hh)
        sel = (lane >= first) & (lane < first + C_PARTS)
        q = _scaled_q(q_ref[0, i * BLOCK:(i + 1) * BLOCK, :])
        qa = jnp.where(_head_mask(hh, BLOCK), q, jnp.where(sel, 1.0, 0.0).astype(q.dtype))
        return _Unit(i, hh, qa, None, s_sc.at[hh])

    _run_units(nb, make_unit, kp_sc, vp_sc, o_ref)


def _fox(qkv3, logf3):
    b, seq, _ = qkv3.shape
    base = 3 * N_PAIRS
    blk = lambda off: pl.BlockSpec((1, seq, LANES), lambda bi, p, off=off: (bi, 0, off + p))
    return pl.pallas_call(
        functools.partial(_fox_kernel, seq=seq),
        grid=(b, N_PAIRS),
        in_specs=[
            blk(base), blk(base + N_PAIRS), blk(base + 2 * N_PAIRS),
            pl.BlockSpec((None, seq, LANES), lambda bi, p: (bi, 0, 0)),
        ],
        out_specs=pl.BlockSpec((1, seq, LANES), lambda bi, p: (bi, 0, p)),
        out_shape=jax.ShapeDtypeStruct((b, seq, GROUP_WIDTH), jnp.bfloat16),
        scratch_shapes=[pltpu.VMEM((seq, LANES), jnp.bfloat16)] + _attn_scratch(seq),
        compiler_params=pltpu.CompilerParams(
            dimension_semantics=("arbitrary", "arbitrary"), vmem_limit_bytes=VMEM_LIMIT),
        name="fox_attn",
    )(qkv3, qkv3, qkv3, logf3)


def _dense_kernel(x_ref, ym_ref, yf_ref, wo_ref, g_ref, wgu_ref, wd_ref, gfin_ref, o_ref, *, final):
    x1 = (x_ref[...]
          + jnp.dot(ym_ref[...], wo_ref[:GROUP_WIDTH, :], preferred_element_type=jnp.float32)
          + jnp.dot(yf_ref[...], wo_ref[GROUP_WIDTH:, :], preferred_element_type=jnp.float32))
    h = _rms(x1, g_ref[...]).astype(jnp.bfloat16)
    acc = jnp.zeros_like(x1)
    for c in range(D_FF // FF_CHUNK):
        lo = c * FF_CHUNK
        gate = jnp.dot(h, wgu_ref[:, lo:lo + FF_CHUNK], preferred_element_type=jnp.float32)
        up = jnp.dot(h, wgu_ref[:, D_FF + lo:D_FF + lo + FF_CHUNK], preferred_element_type=jnp.float32)
        a = (gate / (1.0 + jnp.exp(-gate)) * up).astype(jnp.bfloat16)
        acc = acc + jnp.dot(a, wd_ref[lo:lo + FF_CHUNK, :], preferred_element_type=jnp.float32)
    acc = acc + x1
    if final:
        acc = _rms(acc, gfin_ref[...])
    o_ref[...] = acc


def _dense(x2d, ym, yf, wo, g, wgu, wd, gfin, final):
    t = x2d.shape[0]
    const = lambda shape: pl.BlockSpec(shape, lambda i: (0,) * len(shape),
                                       pipeline_mode=pl.Buffered(1))
    return pl.pallas_call(
        functools.partial(_dense_kernel, final=final),
        grid=(t // TOKEN_TILE,),
        in_specs=[
            pl.BlockSpec((TOKEN_TILE, D_MODEL), lambda i: (i, 0)),
            pl.BlockSpec((TOKEN_TILE, GROUP_WIDTH), lambda i: (i, 0)),
            pl.BlockSpec((TOKEN_TILE, GROUP_WIDTH), lambda i: (i, 0)),
            const((D_MODEL, D_MODEL)),
            const((1, D_MODEL)),
            const((D_MODEL, 2 * D_FF)),
            const((D_FF, D_MODEL)),
            const((1, D_MODEL)),
        ],
        out_specs=pl.BlockSpec((TOKEN_TILE, D_MODEL), lambda i: (i, 0)),
        out_shape=jax.ShapeDtypeStruct((t, D_MODEL), jnp.float32),
        compiler_params=pltpu.CompilerParams(
            dimension_semantics=("arbitrary",), vmem_limit_bytes=VMEM_LIMIT),
        name="dense_ffn",
    )(x2d, ym, yf, wo, g, wgu, wd, gfin)


def kernel(x, w_in, b_f, w_o, g_attn, w_gu, w_down, g_ffn, rel_bias, g_final):
    b, seq, d = x.shape
    depth = w_in.shape[0]
    assert d == D_MODEL and seq % BLOCK == 0 and (b * seq) % TOKEN_TILE == 0
    assert seq // BLOCK <= HEAD_DIM and C_PARTS * N_HEADS <= HEAD_DIM
    bf16 = jnp.bfloat16
    bias = _bias_strips(rel_bias, seq)
    x2d = x.reshape(b * seq, d)
    for layer in range(depth):
        w_qkv = w_in[layer, :, :QKV_WIDTH].astype(bf16)
        w_f = _forget_lane_layout(w_in[layer, :, QKV_WIDTH:]).astype(bf16)
        b_lanes = _forget_lane_layout(b_f[layer][None, :])
        qkv, logf = _inproj(x2d, g_attn[layer][None, :], w_qkv, w_f, b_lanes)
        qkv3 = qkv.reshape(b, seq, QKV_WIDTH)
        y_m = _moba(qkv3, bias)
        y_f = _fox(qkv3, logf.reshape(b, seq, LANES))
        x2d = _dense(x2d, y_m.reshape(b * seq, GROUP_WIDTH), y_f.reshape(b * seq, GROUP_WIDTH),
                     w_o[layer].astype(bf16), g_ffn[layer][None, :], w_gu[layer].astype(bf16),
                     w_down[layer].astype(bf16), g_final[None, :], final=(layer == depth - 1))
    return x2d.reshape(b, seq, d)
```

```python
import functools
import math

import jax
import jax.numpy as jnp
from jax import lax
from jax.experimental import pallas as pl
from jax.experimental.pallas import tpu as pltpu

D_MODEL = 1024
HEAD_DIM = 64
N_HEADS = 8
GROUP_WIDTH = N_HEADS * HEAD_DIM
BLOCK = 256
TOPK = 3
NUM_BUCKETS = 32
MAX_DISTANCE = 1024
D_FF = 2816
RMS_EPS = 1e-6
SCALE = HEAD_DIM ** -0.5
NEG = -1e30

LANES = 128
SUBLANES = 8
BF16_ROWS = 16
HEADS_PER_STEP = LANES // HEAD_DIM
N_PAIRS = N_HEADS // HEADS_PER_STEP
C_PARTS = 3
V_ROWS = HEAD_DIM + BF16_ROWS
FF_CHUNK = 256
TOKEN_TILE = 512
VMEM_LIMIT = 56 * 1024 * 1024

_NT = (((1,), (1,)), ((), ()))
_TN = (((0,), (0,)), ((), ()))


def _rms(x, g):
    return x * lax.rsqrt(jnp.mean(x * x, axis=-1, keepdims=True) + RMS_EPS) * g


def _inproj_kernel(x_ref, g_ref, wk_ref, wqv_ref, wf_ref, bf_ref, k_ref, qvt_ref, logf_ref):
    h = _rms(x_ref[...], g_ref[...]).astype(jnp.bfloat16)
    k_ref[...] = jnp.dot(h, wk_ref[...], preferred_element_type=jnp.float32).astype(jnp.bfloat16)
    qvt_ref[...] = lax.dot_general(wqv_ref[...], h, _NT,
                                   preferred_element_type=jnp.float32).astype(jnp.bfloat16)
    z = jnp.dot(h, wf_ref[...], preferred_element_type=jnp.float32) + bf_ref[...]
    logf_ref[...] = -(jnp.maximum(-z, 0.0) + jnp.log1p(jnp.exp(-jnp.abs(z))))


def _inproj(x2d, g, wk, wqvt, wf, bf):
    t = x2d.shape[0]
    const = lambda shape: pl.BlockSpec(shape, lambda i: (0, 0))
    return pl.pallas_call(
        _inproj_kernel,
        grid=(t // TOKEN_TILE,),
        in_specs=[
            pl.BlockSpec((TOKEN_TILE, D_MODEL), lambda i: (i, 0)),
            const((1, D_MODEL)),
            const((D_MODEL, 2 * GROUP_WIDTH)),
            const((4 * GROUP_WIDTH, D_MODEL)),
            const((D_MODEL, LANES)),
            const((1, LANES)),
        ],
        out_specs=[
            pl.BlockSpec((TOKEN_TILE, 2 * GROUP_WIDTH), lambda i: (i, 0)),
            pl.BlockSpec((4 * GROUP_WIDTH, TOKEN_TILE), lambda i: (0, i)),
            pl.BlockSpec((TOKEN_TILE, LANES), lambda i: (i, 0)),
        ],
        out_shape=[
            jax.ShapeDtypeStruct((t, 2 * GROUP_WIDTH), jnp.bfloat16),
            jax.ShapeDtypeStruct((4 * GROUP_WIDTH, t), jnp.bfloat16),
            jax.ShapeDtypeStruct((t, LANES), jnp.float32),
        ],
        compiler_params=pltpu.CompilerParams(
            dimension_semantics=("arbitrary",), vmem_limit_bytes=VMEM_LIMIT),
        name="inproj",
    )(x2d, g, wk, wqvt, wf, bf)


def _forget_lane_layout(cols):
    rep = jnp.repeat(cols, C_PARTS, axis=-1)
    half = jnp.pad(rep, [(0, 0)] * (cols.ndim - 1) + [(0, HEAD_DIM - C_PARTS * N_HEADS)])
    return jnp.concatenate([half, half], axis=-1)


def _bias_kernel(relt_ref, bkt_ref, out_ref, *, seq):
    h = pl.program_id(0)
    bkt = bkt_ref[...]
    w = jnp.zeros(bkt.shape, jnp.float32)
    for b in range(NUM_BUCKETS):
        w = jnp.where(bkt == b, relt_ref[h, b], w)
    x = jnp.broadcast_to(w, (BLOCK, bkt.shape[1]))
    y = pltpu.roll(x, 0, 1, stride=1, stride_axis=0)
    out_ref[0] = y[:, BLOCK:].T


def _t5_bucket(dist):
    max_exact = NUM_BUCKETS // 2
    is_small = dist < max_exact
    d = jnp.maximum(dist, 1).astype(jnp.float32)
    large = max_exact + (jnp.log(d / max_exact) / math.log(MAX_DISTANCE / max_exact)
                         * (NUM_BUCKETS - max_exact)).astype(jnp.int32)
    large = jnp.minimum(large, NUM_BUCKETS - 1)
    return jnp.where(is_small, dist, large)


def _bias_strips(rel_bias, seq):
    n = jnp.arange(seq + BLOCK, dtype=jnp.int32)
    bkt = _t5_bucket(jnp.clip(seq - n, 0, seq - 1))[None, :]
    return pl.pallas_call(
        functools.partial(_bias_kernel, seq=seq),
        grid=(N_HEADS,),
        in_specs=[
            pl.BlockSpec(memory_space=pltpu.SMEM),
            pl.BlockSpec((1, seq + BLOCK), lambda h: (0, 0)),
        ],
        out_specs=pl.BlockSpec((1, seq, BLOCK), lambda h: (h, 0, 0)),
        out_shape=jax.ShapeDtypeStruct((N_HEADS, seq, BLOCK), jnp.float32),
        compiler_params=pltpu.CompilerParams(dimension_semantics=("arbitrary",)),
        name="moba_bias",
    )(rel_bias.T, bkt)


def _iota(shape, axis):
    return lax.broadcasted_iota(jnp.int32, shape, axis)


def _spare_base(hh):
    return (1 - hh) * HEAD_DIM


def _split3(x):
    p1 = x.astype(jnp.bfloat16)
    r1 = x - p1.astype(jnp.float32)
    p2 = r1.astype(jnp.bfloat16)
    p3 = (r1 - p2.astype(jnp.float32)).astype(jnp.bfloat16)
    return p1, p2, p3


def _augment_kv(k_ref, vt_ref, k_extra, kp_sc, vt_sc):
    seq = k_ref.shape[1]
    lane = _iota((seq, LANES), 1)
    for hh in range(HEADS_PER_STEP):
        hmask = (lane >= hh * HEAD_DIM) & (lane < (hh + 1) * HEAD_DIM)
        kp_sc[hh] = jnp.where(hmask, k_ref[0], k_extra)
        vt_sc[hh, :HEAD_DIM, :] = vt_ref[hh * HEAD_DIM:(hh + 1) * HEAD_DIM, :]
        vt_sc[hh, HEAD_DIM:, :] = jnp.ones((V_ROWS - HEAD_DIM, seq), vt_sc.dtype)


class _Unit:
    def __init__(self, i, hh, qat, tile_term, s_ref):
        self.i, self.hh, self.qat, self.tile_term, self.s_ref = i, hh, qat, tile_term, s_ref
        self.mxp = self.m = self.acc = None


def _score_tile(u, j, kp_sc, causal):
    s = jnp.dot(kp_sc[u.hh, j * BLOCK:(j + 1) * BLOCK, :], u.qat, preferred_element_type=jnp.float32)
    if u.tile_term is not None:
        s = s + u.tile_term(j)
    if j == u.i:
        s = jnp.where(causal, s, NEG)
    u.s_ref[j * BLOCK:(j + 1) * BLOCK, :] = s
    t = jnp.max(s.reshape(BLOCK // SUBLANES, SUBLANES, BLOCK), axis=0)
    u.mxp = t if u.mxp is None else jnp.maximum(u.mxp, t)


def _value_tile(u, j, vt_sc):
    if u.m is None:
        u.m = jnp.max(u.mxp, axis=0, keepdims=True)
    e = jnp.exp(u.s_ref[j * BLOCK:(j + 1) * BLOCK, :] - u.m)
    pv = jnp.dot(vt_sc[u.hh, :, j * BLOCK:(j + 1) * BLOCK], e.astype(jnp.bfloat16),
                 preferred_element_type=jnp.float32)
    u.acc = pv if u.acc is None else u.acc + pv


def _run_units(nb, make_unit, kp_sc, vt_sc, o_ref):
    causal = _iota((BLOCK, BLOCK), 0) <= _iota((BLOCK, BLOCK), 1)
    order = [(i, hh) for i in range(nb) for hh in range(HEADS_PER_STEP)]
    prev = None
    for nxt in order + [None]:
        cur = make_unit(*nxt) if nxt is not None else None
        n1 = cur.i + 1 if cur is not None else 0
        n2 = prev.i + 1 if prev is not None else 0
        for t in range(max(n1, n2)):
            if t < n1:
                _score_tile(cur, t, kp_sc, causal)
            if t < n2:
                _value_tile(prev, t, vt_sc)
        if prev is not None:
            out = prev.acc[:HEAD_DIM, :] * (1.0 / prev.acc[HEAD_DIM:HEAD_DIM + 1, :])
            o_ref[prev.hh * HEAD_DIM:(prev.hh + 1) * HEAD_DIM,
                  prev.i * BLOCK:(prev.i + 1) * BLOCK] = out.astype(o_ref.dtype)
        prev = cur


def _scaled_qt(qt_ref, i):
    q = qt_ref[:, i * BLOCK:(i + 1) * BLOCK]
    return q * jnp.asarray(SCALE, q.dtype)


def _with_spare_rows(qt, hh, spare16):
    head = qt[hh * HEAD_DIM:(hh + 1) * HEAD_DIM, :]
    spare = jnp.concatenate(
        [spare16, jnp.zeros((HEAD_DIM - BF16_ROWS, BLOCK), qt.dtype)], axis=0)
    return jnp.concatenate([head, spare] if hh == 0 else [spare, head], axis=0)


def _attn_scratch(seq):
    return [pltpu.VMEM((HEADS_PER_STEP, seq, BLOCK), jnp.float32),
            pltpu.VMEM((HEADS_PER_STEP, seq, LANES), jnp.bfloat16),
            pltpu.VMEM((HEADS_PER_STEP, V_ROWS, seq), jnp.bfloat16)]


def _moba_kernel(qt_ref, k_ref, vt_ref, bias_ref, o_ref, s_sc, kp_sc, vt_sc, *, seq):
    nb = seq // BLOCK
    key_blk = _iota((seq, LANES), 0) // BLOCK
    onehot = ((_iota((seq, LANES), 1) % HEAD_DIM) == key_blk).astype(k_ref.dtype)
    _augment_kv(k_ref, vt_ref, onehot, kp_sc, vt_sc)
    kmean = jnp.sum(k_ref[0].astype(jnp.float32).reshape(nb, BLOCK, LANES), axis=1) * (1.0 / BLOCK)
    kmean = jnp.concatenate([kmean, jnp.zeros((BF16_ROWS - nb, LANES), jnp.float32)], axis=0)
    blk_row = _iota((BF16_ROWS, BLOCK), 0)
    feat_row = _iota((LANES, BLOCK), 0)

    def make_unit(i, hh):
        qt = _scaled_qt(qt_ref, i)
        pen = jnp.zeros((BF16_ROWS, BLOCK), qt.dtype)
        if i > TOPK:
            in_head = (feat_row >= hh * HEAD_DIM) & (feat_row < (hh + 1) * HEAD_DIM)
            q_head = jnp.where(in_head, qt, jnp.zeros_like(qt)).astype(jnp.float32)
            gate = jnp.dot(kmean, q_head, precision=lax.Precision.HIGHEST,
                           preferred_element_type=jnp.float32)
            rank = jnp.zeros((BF16_ROWS, BLOCK), jnp.int32)
            for jp in range(i):
                other = gate[jp:jp + 1, :]
                beats = (other > gate) | ((other == gate) & (blk_row > jp))
                rank = rank + beats.astype(jnp.int32)
            pen = jnp.where((blk_row < i) & (rank >= TOPK), NEG, 0.0).astype(qt.dtype)

        def tile_term(j):
            off = (nb - 1 - i + j) * BLOCK
            return bias_ref[hh, off:off + BLOCK, :]

        return _Unit(i, hh, _with_spare_rows(qt, hh, pen), tile_term, s_sc.at[hh])

    _run_units(nb, make_unit, kp_sc, vt_sc, o_ref)


def _moba(k3, qvt, bias):
    b, seq, _ = k3.shape
    return pl.pallas_call(
        functools.partial(_moba_kernel, seq=seq),
        grid=(N_PAIRS, b),
        in_specs=[
            pl.BlockSpec((LANES, seq), lambda p, bi: (p, bi)),
            pl.BlockSpec((1, seq, LANES), lambda p, bi: (bi, 0, p)),
            pl.BlockSpec((LANES, seq), lambda p, bi: (2 * N_PAIRS + p, bi)),
            pl.BlockSpec((HEADS_PER_STEP, seq, BLOCK), lambda p, bi: (p, 0, 0)),
        ],
        out_specs=pl.BlockSpec((LANES, seq), lambda p, bi: (p, bi)),
        out_shape=jax.ShapeDtypeStruct((GROUP_WIDTH, b * seq), jnp.bfloat16),
        scratch_shapes=_attn_scratch(seq),
        compiler_params=pltpu.CompilerParams(
            dimension_semantics=("arbitrary", "arbitrary"), vmem_limit_bytes=VMEM_LIMIT),
        name="moba_attn",
    )(qvt, k3, qvt, bias)


def _fox_kernel(qt_ref, k_ref, vt_ref, logf_ref, o_ref, ext_sc, s_sc, kp_sc, vt_sc, *, seq):
    nb = seq // BLOCK
    p = pl.program_id(1)

    @pl.when(p == 0)
    def _():
        tri = (_iota((BLOCK, BLOCK), 1) <= _iota((BLOCK, BLOCK), 0)).astype(jnp.bfloat16)
        lane = _iota((BLOCK, LANES), 1) % HEAD_DIM
        carry = jnp.zeros((1, LANES), jnp.float32)
        for ch in range(nb):
            parts = _split3(logf_ref[ch * BLOCK:(ch + 1) * BLOCK, :])
            cs = jnp.dot(tri, jnp.concatenate(parts, axis=1), preferred_element_type=jnp.float32)
            c = cs[:, :LANES] + cs[:, LANES:2 * LANES] + cs[:, 2 * LANES:] + carry
            carry = c[BLOCK - 1:BLOCK, :]
            p1, p2, p3 = _split3(-c)
            part = lane % C_PARTS
            ext = jnp.where(part == 0, p1, jnp.where(part == 1, p2, p3))
            ext_sc[ch * BLOCK:(ch + 1) * BLOCK, :] = jnp.where(
                lane < C_PARTS * N_HEADS, ext, jnp.zeros_like(ext))

    _augment_kv(k_ref, vt_ref, ext_sc[...], kp_sc, vt_sc)
    row = _iota((BF16_ROWS, BLOCK), 0)

    def make_unit(i, hh):
        qt = _scaled_qt(qt_ref, i)
        first = C_PARTS * (HEADS_PER_STEP * p + hh)
        spare = []
        for grp in range(C_PARTS * N_HEADS // BF16_ROWS + 1):
            r = row + grp * BF16_ROWS
            spare.append(jnp.where((r >= first) & (r < first + C_PARTS), 1.0, 0.0).astype(qt.dtype))
        spare.append(jnp.zeros((HEAD_DIM - len(spare) * BF16_ROWS, BLOCK), qt.dtype))
        spare = jnp.concatenate(spare, axis=0)
        head = qt[hh * HEAD_DIM:(hh + 1) * HEAD_DIM, :]
        qat = jnp.concatenate([head, spare] if hh == 0 else [spare, head], axis=0)
        return _Unit(i, hh, qat, None, s_sc.at[hh])

    _run_units(nb, make_unit, kp_sc, vt_sc, o_ref)


def _fox(k3, qvt, logf3):
    b, seq, _ = k3.shape
    return pl.pallas_call(
        functools.partial(_fox_kernel, seq=seq),
        grid=(b, N_PAIRS),
        in_specs=[
            pl.BlockSpec((LANES, seq), lambda bi, p: (N_PAIRS + p, bi)),
            pl.BlockSpec((1, seq, LANES), lambda bi, p: (bi, 0, N_PAIRS + p)),
            pl.BlockSpec((LANES, seq), lambda bi, p: (3 * N_PAIRS + p, bi)),
            pl.BlockSpec((None, seq, LANES), lambda bi, p: (bi, 0, 0)),
        ],
        out_specs=pl.BlockSpec((LANES, seq), lambda bi, p: (p, bi)),
        out_shape=jax.ShapeDtypeStruct((GROUP_WIDTH, b * seq), jnp.bfloat16),
        scratch_shapes=[pltpu.VMEM((seq, LANES), jnp.bfloat16)] + _attn_scratch(seq),
        compiler_params=pltpu.CompilerParams(
            dimension_semantics=("arbitrary", "arbitrary"), vmem_limit_bytes=VMEM_LIMIT),
        name="fox_attn",
    )(qvt, k3, qvt, logf3)


def _dense_kernel(x_ref, ymt_ref, yft_ref, wo_ref, g_ref, wgu_ref, wd_ref, gfin_ref, o_ref, *, final):
    x1 = (x_ref[...]
          + lax.dot_general(ymt_ref[...], wo_ref[:GROUP_WIDTH, :], _TN,
                            preferred_element_type=jnp.float32)
          + lax.dot_general(yft_ref[...], wo_ref[GROUP_WIDTH:, :], _TN,
                            preferred_element_type=jnp.float32))
    h = _rms(x1, g_ref[...]).astype(jnp.bfloat16)
    acc = jnp.zeros_like(x1)
    for c in range(D_FF // FF_CHUNK):
        lo = c * FF_CHUNK
        gate = jnp.dot(h, wgu_ref[:, lo:lo + FF_CHUNK], preferred_element_type=jnp.float32)
        up = jnp.dot(h, wgu_ref[:, D_FF + lo:D_FF + lo + FF_CHUNK], preferred_element_type=jnp.float32)
        a = (gate / (1.0 + jnp.exp(-gate)) * up).astype(jnp.bfloat16)
        acc = acc + jnp.dot(a, wd_ref[lo:lo + FF_CHUNK, :], preferred_element_type=jnp.float32)
    acc = acc + x1
    if final:
        acc = _rms(acc, gfin_ref[...])
    o_ref[...] = acc


def _dense(x2d, ymt, yft, wo, g, wgu, wd, gfin, final):
    t = x2d.shape[0]
    const = lambda shape: pl.BlockSpec(shape, lambda i: (0,) * len(shape),
                                       pipeline_mode=pl.Buffered(1))
    return pl.pallas_call(
        functools.partial(_dense_kernel, final=final),
        grid=(t // TOKEN_TILE,),
        in_specs=[
            pl.BlockSpec((TOKEN_TILE, D_MODEL), lambda i: (i, 0)),
            pl.BlockSpec((GROUP_WIDTH, TOKEN_TILE), lambda i: (0, i)),
            pl.BlockSpec((GROUP_WIDTH, TOKEN_TILE), lambda i: (0, i)),
            const((D_MODEL, D_MODEL)),
            const((1, D_MODEL)),
            const((D_MODEL, 2 * D_FF)),
            const((D_FF, D_MODEL)),
            const((1, D_MODEL)),
        ],
        out_specs=pl.BlockSpec((TOKEN_TILE, D_MODEL), lambda i: (i, 0)),
        out_shape=jax.ShapeDtypeStruct((t, D_MODEL), jnp.float32),
        compiler_params=pltpu.CompilerParams(
            dimension_semantics=("arbitrary",), vmem_limit_bytes=VMEM_LIMIT),
        name="dense_ffn",
    )(x2d, ymt, yft, wo, g, wgu, wd, gfin)


def kernel(x, w_in, b_f, w_o, g_attn, w_gu, w_down, g_ffn, rel_bias, g_final):
    b, seq, d = x.shape
    depth = w_in.shape[0]
    assert d == D_MODEL and seq % BLOCK == 0 and (b * seq) % TOKEN_TILE == 0
    assert seq // BLOCK <= BF16_ROWS and C_PARTS * N_HEADS <= HEAD_DIM
    bf16 = jnp.bfloat16
    gw = GROUP_WIDTH
    bias = _bias_strips(rel_bias, seq)
    x2d = x.reshape(b * seq, d)
    for layer in range(depth):
        w = w_in[layer]
        w_k = jnp.concatenate([w[:, gw:2 * gw], w[:, 4 * gw:5 * gw]], axis=1).astype(bf16)
        w_qvt = jnp.concatenate([w[:, 0:gw], w[:, 3 * gw:4 * gw], w[:, 2 * gw:3 * gw],
                                 w[:, 5 * gw:6 * gw]], axis=1).T.astype(bf16)
        w_f = _forget_lane_layout(w[:, 6 * gw:]).astype(bf16)
        b_lanes = _forget_lane_layout(b_f[layer][None, :])
        k2, qvt, logf = _inproj(x2d, g_attn[layer][None, :], w_k, w_qvt, w_f, b_lanes)
        k3 = k2.reshape(b, seq, 2 * gw)
        ymt = _moba(k3, qvt, bias)
        yft = _fox(k3, qvt, logf.reshape(b, seq, LANES))
        x2d = _dense(x2d, ymt, yft, w_o[layer].astype(bf16), g_ffn[layer][None, :],
                     w_gu[layer].astype(bf16), w_down[layer].astype(bf16), g_final[None, :],
                     final=(layer == depth - 1))
    return x2d.reshape(b, seq, d)
```

```python
import functools
import math

import jax
import jax.numpy as jnp
from jax import lax
from jax.experimental import pallas as pl
from jax.experimental.pallas import tpu as pltpu

D_MODEL = 1024
HEAD_DIM = 64
N_HEADS = 8
GROUP_WIDTH = N_HEADS * HEAD_DIM
BLOCK = 256
TOPK = 3
NUM_BUCKETS = 32
MAX_DISTANCE = 1024
D_FF = 2816
RMS_EPS = 1e-6
SCALE = HEAD_DIM ** -0.5
NEG = -1e30

LANES = 128
SUBLANES = 8
BF16_ROWS = 16
HEADS_PER_STEP = LANES // HEAD_DIM
N_PAIRS = N_HEADS // HEADS_PER_STEP
C_PARTS = 3
V_ROWS = HEAD_DIM + BF16_ROWS
SLAB = 16
FF_CHUNK = 256
TOKEN_TILE = 512
VMEM_LIMIT = 56 * 1024 * 1024

_NT = (((1,), (1,)), ((), ()))
_TN = (((0,), (0,)), ((), ()))


def _rms(x, g):
    return x * lax.rsqrt(jnp.mean(x * x, axis=-1, keepdims=True) + RMS_EPS) * g


def _inproj_kernel(x_ref, g_ref, wk_ref, wqv_ref, wf_ref, bf_ref, k_ref, qvt_ref, logf_ref):
    h = _rms(x_ref[...], g_ref[...]).astype(jnp.bfloat16)
    k_ref[...] = jnp.dot(h, wk_ref[...], preferred_element_type=jnp.float32).astype(jnp.bfloat16)
    qvt_ref[...] = lax.dot_general(wqv_ref[...], h, _NT,
                                   preferred_element_type=jnp.float32).astype(jnp.bfloat16)
    z = jnp.dot(h, wf_ref[...], preferred_element_type=jnp.float32) + bf_ref[...]
    logf_ref[...] = -(jnp.maximum(-z, 0.0) + jnp.log1p(jnp.exp(-jnp.abs(z))))


def _inproj(x2d, g, wk, wqvt, wf, bf):
    t = x2d.shape[0]
    const = lambda shape: pl.BlockSpec(shape, lambda i: (0, 0))
    return pl.pallas_call(
        _inproj_kernel,
        grid=(t // TOKEN_TILE,),
        in_specs=[
            pl.BlockSpec((TOKEN_TILE, D_MODEL), lambda i: (i, 0)),
            const((1, D_MODEL)),
            const((D_MODEL, 2 * GROUP_WIDTH)),
            const((4 * GROUP_WIDTH, D_MODEL)),
            const((D_MODEL, LANES)),
            const((1, LANES)),
        ],
        out_specs=[
            pl.BlockSpec((TOKEN_TILE, 2 * GROUP_WIDTH), lambda i: (i, 0)),
            pl.BlockSpec((4 * GROUP_WIDTH, TOKEN_TILE), lambda i: (0, i)),
            pl.BlockSpec((TOKEN_TILE, LANES), lambda i: (i, 0)),
        ],
        out_shape=[
            jax.ShapeDtypeStruct((t, 2 * GROUP_WIDTH), jnp.bfloat16),
            jax.ShapeDtypeStruct((4 * GROUP_WIDTH, t), jnp.bfloat16),
            jax.ShapeDtypeStruct((t, LANES), jnp.float32),
        ],
        compiler_params=pltpu.CompilerParams(
            dimension_semantics=("arbitrary",), vmem_limit_bytes=VMEM_LIMIT),
        name="inproj",
    )(x2d, g, wk, wqvt, wf, bf)


def _forget_lane_layout(cols):
    rep = jnp.repeat(cols, C_PARTS, axis=-1)
    half = jnp.pad(rep, [(0, 0)] * (cols.ndim - 1) + [(0, HEAD_DIM - C_PARTS * N_HEADS)])
    return jnp.concatenate([half, half], axis=-1)


def _bias_kernel(relt_ref, bkt_ref, out_ref, *, seq):
    h = pl.program_id(0)
    bkt = bkt_ref[...]
    w = jnp.zeros(bkt.shape, jnp.float32)
    for b in range(NUM_BUCKETS):
        w = jnp.where(bkt == b, relt_ref[h, b], w)
    x = jnp.broadcast_to(w, (BLOCK, bkt.shape[1]))
    y = pltpu.roll(x, 0, 1, stride=1, stride_axis=0)
    out_ref[0] = y[:, BLOCK:].T


def _t5_bucket(dist):
    max_exact = NUM_BUCKETS // 2
    is_small = dist < max_exact
    d = jnp.maximum(dist, 1).astype(jnp.float32)
    large = max_exact + (jnp.log(d / max_exact) / math.log(MAX_DISTANCE / max_exact)
                         * (NUM_BUCKETS - max_exact)).astype(jnp.int32)
    large = jnp.minimum(large, NUM_BUCKETS - 1)
    return jnp.where(is_small, dist, large)


def _bias_strips(rel_bias, seq):
    n = jnp.arange(seq + BLOCK, dtype=jnp.int32)
    bkt = _t5_bucket(jnp.clip(seq - n, 0, seq - 1))[None, :]
    return pl.pallas_call(
        functools.partial(_bias_kernel, seq=seq),
        grid=(N_HEADS,),
        in_specs=[
            pl.BlockSpec(memory_space=pltpu.SMEM),
            pl.BlockSpec((1, seq + BLOCK), lambda h: (0, 0)),
        ],
        out_specs=pl.BlockSpec((1, seq, BLOCK), lambda h: (h, 0, 0)),
        out_shape=jax.ShapeDtypeStruct((N_HEADS, seq, BLOCK), jnp.float32),
        compiler_params=pltpu.CompilerParams(dimension_semantics=("arbitrary",)),
        name="moba_bias",
    )(rel_bias.T, bkt)


def _iota(shape, axis):
    return lax.broadcasted_iota(jnp.int32, shape, axis)


def _spare_base(hh):
    return (1 - hh) * HEAD_DIM


def _split3(x):
    p1 = x.astype(jnp.bfloat16)
    r1 = x - p1.astype(jnp.float32)
    p2 = r1.astype(jnp.bfloat16)
    p3 = (r1 - p2.astype(jnp.float32)).astype(jnp.bfloat16)
    return p1, p2, p3


def _augment_kv(k_ref, vt_ref, k_extra, kp_sc, vt_sc):
    seq = k_ref.shape[1]
    lane = _iota((seq, LANES), 1)
    for hh in range(HEADS_PER_STEP):
        hmask = (lane >= hh * HEAD_DIM) & (lane < (hh + 1) * HEAD_DIM)
        kp_sc[hh] = jnp.where(hmask, k_ref[0], k_extra)
        vt_sc[hh, :HEAD_DIM, :] = vt_ref[hh * HEAD_DIM:(hh + 1) * HEAD_DIM, :]
        vt_sc[hh, HEAD_DIM:, :] = jnp.ones((V_ROWS - HEAD_DIM, seq), vt_sc.dtype)


class _Unit:
    def __init__(self, i, hh, qat, tile_term, s_ref):
        self.i, self.hh, self.qat, self.tile_term, self.s_ref = i, hh, qat, tile_term, s_ref
        self.mxp = self.m = self.acc = None


def _score_tile(u, j, kp_sc):
    key_in_tile = _iota((SLAB, BLOCK), 0)
    query_in_tile = _iota((SLAB, BLOCK), 1)
    s = jnp.dot(kp_sc[u.hh, j * BLOCK:(j + 1) * BLOCK, :], u.qat, preferred_element_type=jnp.float32)
    for r in range(0, BLOCK, SLAB):
        piece = s[r:r + SLAB, :]
        if u.tile_term is not None:
            piece = piece + u.tile_term(j, r)
        if j == u.i:
            piece = jnp.where(key_in_tile + r <= query_in_tile, piece, NEG)
        u.s_ref[j * BLOCK + r:j * BLOCK + r + SLAB, :] = piece
        t = jnp.max(piece.reshape(SLAB // SUBLANES, SUBLANES, BLOCK), axis=0)
        u.mxp = t if u.mxp is None else jnp.maximum(u.mxp, t)


def _value_tile(u, j, vt_sc):
    if u.m is None:
        u.m = jnp.max(u.mxp, axis=0, keepdims=True)
    e = jnp.concatenate(
        [jnp.exp(u.s_ref[j * BLOCK + r:j * BLOCK + r + SLAB, :] - u.m).astype(jnp.bfloat16)
         for r in range(0, BLOCK, SLAB)], axis=0)
    pv = jnp.dot(vt_sc[u.hh, :, j * BLOCK:(j + 1) * BLOCK], e, preferred_element_type=jnp.float32)
    u.acc = pv if u.acc is None else u.acc + pv


def _run_units(nb, make_unit, kp_sc, vt_sc, o_ref):
    groups = [(a, nb - 1 - a) for a in range(nb // 2)]
    prev = []
    for grp in groups + [None]:
        cur = ([make_unit(i, hh) for i in grp for hh in range(HEADS_PER_STEP)]
               if grp is not None else [])
        for t in range(nb):
            for u in cur:
                if t <= u.i:
                    _score_tile(u, t, kp_sc)
            for u in prev:
                if t <= u.i:
                    _value_tile(u, t, vt_sc)
        for u in prev:
            out = u.acc[:HEAD_DIM, :] * (1.0 / u.acc[HEAD_DIM:HEAD_DIM + 1, :])
            o_ref[u.hh * HEAD_DIM:(u.hh + 1) * HEAD_DIM,
                  u.i * BLOCK:(u.i + 1) * BLOCK] = out.astype(o_ref.dtype)
        prev = cur


def _scaled_qt(qt_ref, i):
    q = qt_ref[:, i * BLOCK:(i + 1) * BLOCK]
    return q * jnp.asarray(SCALE, q.dtype)


def _with_spare_rows(qt, hh, spare16):
    head = qt[hh * HEAD_DIM:(hh + 1) * HEAD_DIM, :]
    spare = jnp.concatenate(
        [spare16, jnp.zeros((HEAD_DIM - BF16_ROWS, BLOCK), qt.dtype)], axis=0)
    return jnp.concatenate([head, spare] if hh == 0 else [spare, head], axis=0)


def _score_slot(i, hh, nb):
    a = min(i, nb - 1 - i)
    return ((a % 2) * 2 + (i != a)) * HEADS_PER_STEP + hh


def _attn_scratch(seq):
    return [pltpu.VMEM((4 * HEADS_PER_STEP, seq, BLOCK), jnp.float32),
            pltpu.VMEM((HEADS_PER_STEP, seq, LANES), jnp.bfloat16),
            pltpu.VMEM((HEADS_PER_STEP, V_ROWS, seq), jnp.bfloat16)]


def _moba_kernel(qt_ref, k_ref, vt_ref, bias_ref, o_ref, s_sc, kp_sc, vt_sc, *, seq):
    nb = seq // BLOCK
    key_blk = _iota((seq, LANES), 0) // BLOCK
    onehot = ((_iota((seq, LANES), 1) % HEAD_DIM) == key_blk).astype(k_ref.dtype)
    _augment_kv(k_ref, vt_ref, onehot, kp_sc, vt_sc)
    kmean = jnp.sum(k_ref[0].astype(jnp.float32).reshape(nb, BLOCK, LANES), axis=1) * (1.0 / BLOCK)
    kmean = jnp.concatenate([kmean, jnp.zeros((BF16_ROWS - nb, LANES), jnp.float32)], axis=0)
    blk_row = _iota((BF16_ROWS, BLOCK), 0)
    feat_row = _iota((LANES, BLOCK), 0)

    def make_unit(i, hh):
        qt = _scaled_qt(qt_ref, i)
        pen = jnp.zeros((BF16_ROWS, BLOCK), qt.dtype)
        if i > TOPK:
            in_head = (feat_row >= hh * HEAD_DIM) & (feat_row < (hh + 1) * HEAD_DIM)
            q_head = jnp.where(in_head, qt, jnp.zeros_like(qt)).astype(jnp.float32)
            gate = jnp.dot(kmean, q_head, precision=lax.Precision.HIGHEST,
                           preferred_element_type=jnp.float32)
            rank = jnp.zeros((BF16_ROWS, BLOCK), jnp.int32)
            for jp in range(i):
                other = gate[jp:jp + 1, :]
                beats = (other > gate) | ((other == gate) & (blk_row > jp))
                rank = rank + beats.astype(jnp.int32)
            pen = jnp.where((blk_row < i) & (rank >= TOPK), NEG, 0.0).astype(qt.dtype)

        def tile_term(j, r):
            off = (nb - 1 - i + j) * BLOCK + r
            return bias_ref[hh, off:off + SLAB, :]

        return _Unit(i, hh, _with_spare_rows(qt, hh, pen), tile_term, s_sc.at[_score_slot(i, hh, nb)])

    _run_units(nb, make_unit, kp_sc, vt_sc, o_ref)


def _moba(k3, qvt, bias):
    b, seq, _ = k3.shape
    return pl.pallas_call(
        functools.partial(_moba_kernel, seq=seq),
        grid=(N_PAIRS, b),
        in_specs=[
            pl.BlockSpec((LANES, seq), lambda p, bi: (p, bi)),
            pl.BlockSpec((1, seq, LANES), lambda p, bi: (bi, 0, p)),
            pl.BlockSpec((LANES, seq), lambda p, bi: (2 * N_PAIRS + p, bi)),
            pl.BlockSpec((HEADS_PER_STEP, seq, BLOCK), lambda p, bi: (p, 0, 0)),
        ],
        out_specs=pl.BlockSpec((LANES, seq), lambda p, bi: (p, bi)),
        out_shape=jax.ShapeDtypeStruct((GROUP_WIDTH, b * seq), jnp.bfloat16),
        scratch_shapes=_attn_scratch(seq),
        compiler_params=pltpu.CompilerParams(
            dimension_semantics=("arbitrary", "arbitrary"), vmem_limit_bytes=VMEM_LIMIT),
        name="moba_attn",
    )(qvt, k3, qvt, bias)


def _fox_kernel(qt_ref, k_ref, vt_ref, logf_ref, o_ref, ext_sc, s_sc, kp_sc, vt_sc, *, seq):
    nb = seq // BLOCK
    p = pl.program_id(1)

    @pl.when(p == 0)
    def _():
        tri = (_iota((BLOCK, BLOCK), 1) <= _iota((BLOCK, BLOCK), 0)).astype(jnp.bfloat16)
        lane = _iota((BLOCK, LANES), 1) % HEAD_DIM
        carry = jnp.zeros((1, LANES), jnp.float32)
        for ch in range(nb):
            parts = _split3(logf_ref[ch * BLOCK:(ch + 1) * BLOCK, :])
            cs = jnp.dot(tri, jnp.concatenate(parts, axis=1), preferred_element_type=jnp.float32)
            c = cs[:, :LANES] + cs[:, LANES:2 * LANES] + cs[:, 2 * LANES:] + carry
            carry = c[BLOCK - 1:BLOCK, :]
            p1, p2, p3 = _split3(-c)
            part = lane % C_PARTS
            ext = jnp.where(part == 0, p1, jnp.where(part == 1, p2, p3))
            ext_sc[ch * BLOCK:(ch + 1) * BLOCK, :] = jnp.where(
                lane < C_PARTS * N_HEADS, ext, jnp.zeros_like(ext))

    _augment_kv(k_ref, vt_ref, ext_sc[...], kp_sc, vt_sc)
    row = _iota((BF16_ROWS, BLOCK), 0)

    def make_unit(i, hh):
        qt = _scaled_qt(qt_ref, i)
        first = C_PARTS * (HEADS_PER_STEP * p + hh)
        spare = []
        for grp in range(C_PARTS * N_HEADS // BF16_ROWS + 1):
            r = row + grp * BF16_ROWS
            spare.append(jnp.where((r >= first) & (r < first + C_PARTS), 1.0, 0.0).astype(qt.dtype))
        spare.append(jnp.zeros((HEAD_DIM - len(spare) * BF16_ROWS, BLOCK), qt.dtype))
        spare = jnp.concatenate(spare, axis=0)
        head = qt[hh * HEAD_DIM:(hh + 1) * HEAD_DIM, :]
        qat = jnp.concatenate([head, spare] if hh == 0 else [spare, head], axis=0)
        return _Unit(i, hh, qat, None, s_sc.at[_score_slot(i, hh, nb)])

    _run_units(nb, make_unit, kp_sc, vt_sc, o_ref)


def _fox(k3, qvt, logf3):
    b, seq, _ = k3.shape
    return pl.pallas_call(
        functools.partial(_fox_kernel, seq=seq),
        grid=(b, N_PAIRS),
        in_specs=[
            pl.BlockSpec((LANES, seq), lambda bi, p: (N_PAIRS + p, bi)),
            pl.BlockSpec((1, seq, LANES), lambda bi, p: (bi, 0, N_PAIRS + p)),
            pl.BlockSpec((LANES, seq), lambda bi, p: (3 * N_PAIRS + p, bi)),
            pl.BlockSpec((None, seq, LANES), lambda bi, p: (bi, 0, 0)),
        ],
        out_specs=pl.BlockSpec((LANES, seq), lambda bi, p: (p, bi)),
        out_shape=jax.ShapeDtypeStruct((GROUP_WIDTH, b * seq), jnp.bfloat16),
        scratch_shapes=[pltpu.VMEM((seq, LANES), jnp.bfloat16)] + _attn_scratch(seq),
        compiler_params=pltpu.CompilerParams(
            dimension_semantics=("arbitrary", "arbitrary"), vmem_limit_bytes=VMEM_LIMIT),
        name="fox_attn",
    )(qvt, k3, qvt, logf3)


def _dense_kernel(x_ref, ymt_ref, yft_ref, wo_ref, g_ref, wgu_ref, wd_ref, gfin_ref, o_ref, *, final):
    x1 = (x_ref[...]
          + lax.dot_general(ymt_ref[...], wo_ref[:GROUP_WIDTH, :], _TN,
                            preferred_element_type=jnp.float32)
          + lax.dot_general(yft_ref[...], wo_ref[GROUP_WIDTH:, :], _TN,
                            preferred_element_type=jnp.float32))
    h = _rms(x1, g_ref[...]).astype(jnp.bfloat16)
    acc = jnp.zeros_like(x1)
    for c in range(D_FF // FF_CHUNK):
        lo = c * FF_CHUNK
        gate = jnp.dot(h, wgu_ref[:, lo:lo + FF_CHUNK], preferred_element_type=jnp.float32)
        up = jnp.dot(h, wgu_ref[:, D_FF + lo:D_FF + lo + FF_CHUNK], preferred_element_type=jnp.float32)
        a = (gate / (1.0 + jnp.exp(-gate)) * up).astype(jnp.bfloat16)
        acc = acc + jnp.dot(a, wd_ref[lo:lo + FF_CHUNK, :], preferred_element_type=jnp.float32)
    acc = acc + x1
    if final:
        acc = _rms(acc, gfin_ref[...])
    o_ref[...] = acc


def _dense(x2d, ymt, yft, wo, g, wgu, wd, gfin, final):
    t = x2d.shape[0]
    const = lambda shape: pl.BlockSpec(shape, lambda i: (0,) * len(shape),
                                       pipeline_mode=pl.Buffered(1))
    return pl.pallas_call(
        functools.partial(_dense_kernel, final=final),
        grid=(t // TOKEN_TILE,),
        in_specs=[
            pl.BlockSpec((TOKEN_TILE, D_MODEL), lambda i: (i, 0)),
            pl.BlockSpec((GROUP_WIDTH, TOKEN_TILE), lambda i: (0, i)),
            pl.BlockSpec((GROUP_WIDTH, TOKEN_TILE), lambda i: (0, i)),
            const((D_MODEL, D_MODEL)),
            const((1, D_MODEL)),
            const((D_MODEL, 2 * D_FF)),
            const((D_FF, D_MODEL)),
            const((1, D_MODEL)),
        ],
        out_specs=pl.BlockSpec((TOKEN_TILE, D_MODEL), lambda i: (i, 0)),
        out_shape=jax.ShapeDtypeStruct((t, D_MODEL), jnp.float32),
        compiler_params=pltpu.CompilerParams(
            dimension_semantics=("arbitrary",), vmem_limit_bytes=VMEM_LIMIT),
        name="dense_ffn",
    )(x2d, ymt, yft, wo, g, wgu, wd, gfin)


def kernel(x, w_in, b_f, w_o, g_attn, w_gu, w_down, g_ffn, rel_bias, g_final):
    b, seq, d = x.shape
    depth = w_in.shape[0]
    assert d == D_MODEL and seq % BLOCK == 0 and (b * seq) % TOKEN_TILE == 0
    assert seq // BLOCK <= BF16_ROWS and C_PARTS * N_HEADS <= HEAD_DIM
    bf16 = jnp.bfloat16
    gw = GROUP_WIDTH
    bias = _bias_strips(rel_bias, seq)
    x2d = x.reshape(b * seq, d)
    for layer in range(depth):
        w = w_in[layer]
        w_k = jnp.concatenate([w[:, gw:2 * gw], w[:, 4 * gw:5 * gw]], axis=1).astype(bf16)
        w_qvt = jnp.concatenate([w[:, 0:gw], w[:, 3 * gw:4 * gw], w[:, 2 * gw:3 * gw],
                                 w[:, 5 * gw:6 * gw]], axis=1).T.astype(bf16)
        w_f = _forget_lane_layout(w[:, 6 * gw:]).astype(bf16)
        b_lanes = _forget_lane_layout(b_f[layer][None, :])
        k2, qvt, logf = _inproj(x2d, g_attn[layer][None, :], w_k, w_qvt, w_f, b_lanes)
        k3 = k2.reshape(b, seq, 2 * gw)
        ymt = _moba(k3, qvt, bias)
        yft = _fox(k3, qvt, logf.reshape(b, seq, LANES))
        x2d = _dense(x2d, ymt, yft, w_o[layer].astype(bf16), g_ffn[layer][None, :],
                     w_gu[layer].astype(bf16), w_down[layer].astype(bf16), g_final[None, :],
                     final=(layer == depth - 1))
    return x2d.reshape(b, seq, d)
```

```python
import functools
import math

import jax
import jax.numpy as jnp
from jax import lax
from jax.experimental import pallas as pl
from jax.experimental.pallas import tpu as pltpu

D_MODEL = 1024
HEAD_DIM = 64
N_HEADS = 8
GROUP_WIDTH = N_HEADS * HEAD_DIM
BLOCK = 256
TOPK = 3
NUM_BUCKETS = 32
MAX_DISTANCE = 1024
D_FF = 2816
RMS_EPS = 1e-6
SCALE = HEAD_DIM ** -0.5
NEG = -1e30

LANES = 128
SUBLANES = 8
BF16_ROWS = 16
HEADS_PER_PAIR = LANES // HEAD_DIM
PAIRS_PER_STEP = 2
HEADS_IN_STEP = PAIRS_PER_STEP * HEADS_PER_PAIR
STEP_WIDTH = PAIRS_PER_STEP * LANES
N_STEPS = GROUP_WIDTH // STEP_WIDTH
UNITS_PER_GROUP = 2 * HEADS_PER_PAIR
C_PARTS = 3
V_ROWS = HEAD_DIM + BF16_ROWS
SLAB = 16
FF_CHUNK = 256
TOKEN_TILE = 512
VMEM_LIMIT = 56 * 1024 * 1024

_NT = (((1,), (1,)), ((), ()))
_TN = (((0,), (0,)), ((), ()))


def _rms(x, g):
    return x * lax.rsqrt(jnp.mean(x * x, axis=-1, keepdims=True) + RMS_EPS) * g


def _inproj_kernel(x_ref, g_ref, wk_ref, wqv_ref, wf_ref, bf_ref, k_ref, qvt_ref, logf_ref):
    h = _rms(x_ref[...], g_ref[...]).astype(jnp.bfloat16)
    k_ref[...] = jnp.dot(h, wk_ref[...], preferred_element_type=jnp.float32).astype(jnp.bfloat16)
    qvt_ref[...] = lax.dot_general(wqv_ref[...], h, _NT,
                                   preferred_element_type=jnp.float32).astype(jnp.bfloat16)
    z = jnp.dot(h, wf_ref[...], preferred_element_type=jnp.float32) + bf_ref[...]
    logf_ref[...] = -(jnp.maximum(-z, 0.0) + jnp.log1p(jnp.exp(-jnp.abs(z))))


def _inproj(x2d, g, wk, wqvt, wf, bf):
    t = x2d.shape[0]
    const = lambda shape: pl.BlockSpec(shape, lambda i: (0, 0))
    return pl.pallas_call(
        _inproj_kernel,
        grid=(t // TOKEN_TILE,),
        in_specs=[
            pl.BlockSpec((TOKEN_TILE, D_MODEL), lambda i: (i, 0)),
            const((1, D_MODEL)),
            const((D_MODEL, 2 * GROUP_WIDTH)),
            const((4 * GROUP_WIDTH, D_MODEL)),
            const((D_MODEL, LANES)),
            const((1, LANES)),
        ],
        out_specs=[
            pl.BlockSpec((TOKEN_TILE, 2 * GROUP_WIDTH), lambda i: (i, 0)),
            pl.BlockSpec((4 * GROUP_WIDTH, TOKEN_TILE), lambda i: (0, i)),
            pl.BlockSpec((TOKEN_TILE, LANES), lambda i: (i, 0)),
        ],
        out_shape=[
            jax.ShapeDtypeStruct((t, 2 * GROUP_WIDTH), jnp.bfloat16),
            jax.ShapeDtypeStruct((4 * GROUP_WIDTH, t), jnp.bfloat16),
            jax.ShapeDtypeStruct((t, LANES), jnp.float32),
        ],
        compiler_params=pltpu.CompilerParams(
            dimension_semantics=("arbitrary",), vmem_limit_bytes=VMEM_LIMIT),
        name="inproj",
    )(x2d, g, wk, wqvt, wf, bf)


def _forget_lane_layout(cols):
    rep = jnp.repeat(cols, C_PARTS, axis=-1)
    half = jnp.pad(rep, [(0, 0)] * (cols.ndim - 1) + [(0, HEAD_DIM - C_PARTS * N_HEADS)])
    return jnp.concatenate([half, half], axis=-1)


def _bias_kernel(relt_ref, bkt_ref, out_ref, *, seq):
    h = pl.program_id(0)
    bkt = bkt_ref[...]
    w = jnp.zeros(bkt.shape, jnp.float32)
    for b in range(NUM_BUCKETS):
        w = jnp.where(bkt == b, relt_ref[h, b], w)
    x = jnp.broadcast_to(w, (BLOCK, bkt.shape[1]))
    y = pltpu.roll(x, 0, 1, stride=1, stride_axis=0)
    out_ref[0] = y[:, BLOCK:].T


def _t5_bucket(dist):
    max_exact = NUM_BUCKETS // 2
    is_small = dist < max_exact
    d = jnp.maximum(dist, 1).astype(jnp.float32)
    large = max_exact + (jnp.log(d / max_exact) / math.log(MAX_DISTANCE / max_exact)
                         * (NUM_BUCKETS - max_exact)).astype(jnp.int32)
    large = jnp.minimum(large, NUM_BUCKETS - 1)
    return jnp.where(is_small, dist, large)


def _bias_strips(rel_bias, seq):
    n = jnp.arange(seq + BLOCK, dtype=jnp.int32)
    bkt = _t5_bucket(jnp.clip(seq - n, 0, seq - 1))[None, :]
    return pl.pallas_call(
        functools.partial(_bias_kernel, seq=seq),
        grid=(N_HEADS,),
        in_specs=[
            pl.BlockSpec(memory_space=pltpu.SMEM),
            pl.BlockSpec((1, seq + BLOCK), lambda h: (0, 0)),
        ],
        out_specs=pl.BlockSpec((1, seq, BLOCK), lambda h: (h, 0, 0)),
        out_shape=jax.ShapeDtypeStruct((N_HEADS, seq, BLOCK), jnp.float32),
        compiler_params=pltpu.CompilerParams(dimension_semantics=("arbitrary",)),
        name="moba_bias",
    )(rel_bias.T, bkt)


def _iota(shape, axis):
    return lax.broadcasted_iota(jnp.int32, shape, axis)


def _split3(x):
    p1 = x.astype(jnp.bfloat16)
    r1 = x - p1.astype(jnp.float32)
    p2 = r1.astype(jnp.bfloat16)
    p3 = (r1 - p2.astype(jnp.float32)).astype(jnp.bfloat16)
    return p1, p2, p3


def _augment_kv(k_ref, vt_ref, k_extra, kp_sc, vt_sc):
    seq = k_ref.shape[1]
    lane = _iota((seq, LANES), 1)
    for h in range(HEADS_IN_STEP):
        pr, hh = divmod(h, HEADS_PER_PAIR)
        hmask = (lane >= hh * HEAD_DIM) & (lane < (hh + 1) * HEAD_DIM)
        kp_sc[h] = jnp.where(hmask, k_ref[0, :, pr * LANES:(pr + 1) * LANES], k_extra)
        vt_sc[h, :HEAD_DIM, :] = vt_ref[h * HEAD_DIM:(h + 1) * HEAD_DIM, :]
        vt_sc[h, HEAD_DIM:, :] = jnp.ones((V_ROWS - HEAD_DIM, seq), vt_sc.dtype)


class _Unit:
    def __init__(self, h, i, qat, tile_term):
        self.h, self.i, self.qat, self.tile_term = h, i, qat, tile_term
        self.s_ref = self.mxp = self.m = self.acc = None


def _score_tile(u, j, kp_sc):
    key_in_tile = _iota((SLAB, BLOCK), 0)
    query_in_tile = _iota((SLAB, BLOCK), 1)
    s = jnp.dot(kp_sc[u.h, j * BLOCK:(j + 1) * BLOCK, :], u.qat, preferred_element_type=jnp.float32)
    for r in range(0, BLOCK, SLAB):
        piece = s[r:r + SLAB, :]
        if u.tile_term is not None:
            piece = piece + u.tile_term(j, r)
        if j == u.i:
            piece = jnp.where(key_in_tile + r <= query_in_tile, piece, NEG)
        u.s_ref[j * BLOCK + r:j * BLOCK + r + SLAB, :] = piece
        t = jnp.max(piece.reshape(SLAB // SUBLANES, SUBLANES, BLOCK), axis=0)
        u.mxp = t if u.mxp is None else jnp.maximum(u.mxp, t)


def _value_tile(u, j, vt_sc):
    if u.m is None:
        u.m = jnp.max(u.mxp, axis=0, keepdims=True)
    e = jnp.concatenate(
        [jnp.exp(u.s_ref[j * BLOCK + r:j * BLOCK + r + SLAB, :] - u.m).astype(jnp.bfloat16)
         for r in range(0, BLOCK, SLAB)], axis=0)
    pv = jnp.dot(vt_sc[u.h, :, j * BLOCK:(j + 1) * BLOCK], e, preferred_element_type=jnp.float32)
    u.acc = pv if u.acc is None else u.acc + pv


def _run_units(nb, make_unit, s_sc, kp_sc, vt_sc, o_ref):
    groups = [[(pr * HEADS_PER_PAIR + hh, i) for i in (a, nb - 1 - a) for hh in range(HEADS_PER_PAIR)]
              for pr in range(PAIRS_PER_STEP) for a in range(nb // 2)]
    prev = []
    for g, grp in enumerate(groups + [None]):
        cur = [make_unit(h, i) for h, i in grp] if grp is not None else []
        for n, u in enumerate(cur):
            u.s_ref = s_sc.at[(g % 2) * UNITS_PER_GROUP + n]
        for t in range(nb):
            for u in cur:
                if t <= u.i:
                    _score_tile(u, t, kp_sc)
            for u in prev:
                if t <= u.i:
                    _value_tile(u, t, vt_sc)
        for u in prev:
            out = u.acc[:HEAD_DIM, :] * (1.0 / u.acc[HEAD_DIM:HEAD_DIM + 1, :])
            o_ref[u.h * HEAD_DIM:(u.h + 1) * HEAD_DIM,
                  u.i * BLOCK:(u.i + 1) * BLOCK] = out.astype(o_ref.dtype)
        prev = cur


def _scaled_qt(qt_ref, h, i):
    pr = h // HEADS_PER_PAIR
    q = qt_ref[pr * LANES:(pr + 1) * LANES, i * BLOCK:(i + 1) * BLOCK]
    return q * jnp.asarray(SCALE, q.dtype)


def _with_spare_rows(qt, hh, spare):
    head = qt[hh * HEAD_DIM:(hh + 1) * HEAD_DIM, :]
    spare = jnp.concatenate(
        [spare, jnp.zeros((HEAD_DIM - spare.shape[0], BLOCK), qt.dtype)], axis=0)
    return jnp.concatenate([head, spare] if hh == 0 else [spare, head], axis=0)


def _attn_scratch(seq):
    return [pltpu.VMEM((2 * UNITS_PER_GROUP, seq, BLOCK), jnp.float32),
            pltpu.VMEM((HEADS_IN_STEP, seq, LANES), jnp.bfloat16),
            pltpu.VMEM((HEADS_IN_STEP, V_ROWS, seq), jnp.bfloat16)]


def _moba_kernel(qt_ref, k_ref, vt_ref, bias_ref, o_ref, s_sc, kp_sc, vt_sc, *, seq):
    nb = seq // BLOCK
    key_blk = _iota((seq, LANES), 0) // BLOCK
    onehot = ((_iota((seq, LANES), 1) % HEAD_DIM) == key_blk).astype(k_ref.dtype)
    _augment_kv(k_ref, vt_ref, onehot, kp_sc, vt_sc)
    blk_row = _iota((BF16_ROWS, BLOCK), 0)
    feat_row = _iota((LANES, BLOCK), 0)

    def pair_kmean(pr):
        k = k_ref[0, :, pr * LANES:(pr + 1) * LANES].astype(jnp.float32)
        kmean = jnp.sum(k.reshape(nb, BLOCK, LANES), axis=1) * (1.0 / BLOCK)
        return jnp.concatenate([kmean, jnp.zeros((BF16_ROWS - nb, LANES), jnp.float32)], axis=0)

    kmeans = [pair_kmean(pr) for pr in range(PAIRS_PER_STEP)]

    def make_unit(h, i):
        pr, hh = divmod(h, HEADS_PER_PAIR)
        qt = _scaled_qt(qt_ref, h, i)
        pen = jnp.zeros((BF16_ROWS, BLOCK), qt.dtype)
        if i > TOPK:
            in_head = (feat_row >= hh * HEAD_DIM) & (feat_row < (hh + 1) * HEAD_DIM)
            q_head = jnp.where(in_head, qt, jnp.zeros_like(qt)).astype(jnp.float32)
            gate = jnp.dot(kmeans[pr], q_head, precision=lax.Precision.HIGHEST,
                           preferred_element_type=jnp.float32)
            rank = jnp.zeros((BF16_ROWS, BLOCK), jnp.int32)
            for jp in range(i):
                other = gate[jp:jp + 1, :]
                beats = (other > gate) | ((other == gate) & (blk_row > jp))
                rank = rank + beats.astype(jnp.int32)
            pen = jnp.where((blk_row < i) & (rank >= TOPK), NEG, 0.0).astype(qt.dtype)

        def tile_term(j, r):
            off = (nb - 1 - i + j) * BLOCK + r
            return bias_ref[h, off:off + SLAB, :]

        return _Unit(h, i, _with_spare_rows(qt, hh, pen), tile_term)

    _run_units(nb, make_unit, s_sc, kp_sc, vt_sc, o_ref)


def _moba(k3, qvt, bias):
    b, seq, _ = k3.shape
    return pl.pallas_call(
        functools.partial(_moba_kernel, seq=seq),
        grid=(N_STEPS, b),
        in_specs=[
            pl.BlockSpec((STEP_WIDTH, seq), lambda p, bi: (p, bi)),
            pl.BlockSpec((1, seq, STEP_WIDTH), lambda p, bi: (bi, 0, p)),
            pl.BlockSpec((STEP_WIDTH, seq), lambda p, bi: (2 * N_STEPS + p, bi)),
            pl.BlockSpec((HEADS_IN_STEP, seq, BLOCK), lambda p, bi: (p, 0, 0),
                         pipeline_mode=pl.Buffered(1)),
        ],
        out_specs=pl.BlockSpec((STEP_WIDTH, seq), lambda p, bi: (p, bi)),
        out_shape=jax.ShapeDtypeStruct((GROUP_WIDTH, b * seq), jnp.bfloat16),
        scratch_shapes=_attn_scratch(seq),
        compiler_params=pltpu.CompilerParams(
            dimension_semantics=("arbitrary", "arbitrary"), vmem_limit_bytes=VMEM_LIMIT),
        name="moba_attn",
    )(qvt, k3, qvt, bias)


def _fox_kernel(qt_ref, k_ref, vt_ref, logf_ref, o_ref, ext_sc, s_sc, kp_sc, vt_sc, *, seq):
    nb = seq // BLOCK
    p = pl.program_id(1)

    @pl.when(p == 0)
    def _():
        tri = (_iota((BLOCK, BLOCK), 1) <= _iota((BLOCK, BLOCK), 0)).astype(jnp.bfloat16)
        lane = _iota((BLOCK, LANES), 1) % HEAD_DIM
        carry = jnp.zeros((1, LANES), jnp.float32)
        for ch in range(nb):
            parts = _split3(logf_ref[ch * BLOCK:(ch + 1) * BLOCK, :])
            cs = jnp.dot(tri, jnp.concatenate(parts, axis=1), preferred_element_type=jnp.float32)
            c = cs[:, :LANES] + cs[:, LANES:2 * LANES] + cs[:, 2 * LANES:] + carry
            carry = c[BLOCK - 1:BLOCK, :]
            p1, p2, p3 = _split3(-c)
            part = lane % C_PARTS
            ext = jnp.where(part == 0, p1, jnp.where(part == 1, p2, p3))
            ext_sc[ch * BLOCK:(ch + 1) * BLOCK, :] = jnp.where(
                lane < C_PARTS * N_HEADS, ext, jnp.zeros_like(ext))

    _augment_kv(k_ref, vt_ref, ext_sc[...], kp_sc, vt_sc)
    n_spare = -(-C_PARTS * N_HEADS // BF16_ROWS) * BF16_ROWS
    row = _iota((n_spare, BLOCK), 0)

    def make_unit(h, i):
        qt = _scaled_qt(qt_ref, h, i)
        first = C_PARTS * (HEADS_IN_STEP * p + h)
        spare = jnp.where((row >= first) & (row < first + C_PARTS), 1.0, 0.0).astype(qt.dtype)
        return _Unit(h, i, _with_spare_rows(qt, h % HEADS_PER_PAIR, spare), None)

    _run_units(nb, make_unit, s_sc, kp_sc, vt_sc, o_ref)


def _fox(k3, qvt, logf3):
    b, seq, _ = k3.shape
    return pl.pallas_call(
        functools.partial(_fox_kernel, seq=seq),
        grid=(b, N_STEPS),
        in_specs=[
            pl.BlockSpec((STEP_WIDTH, seq), lambda bi, p: (N_STEPS + p, bi)),
            pl.BlockSpec((1, seq, STEP_WIDTH), lambda bi, p: (bi, 0, N_STEPS + p)),
            pl.BlockSpec((STEP_WIDTH, seq), lambda bi, p: (3 * N_STEPS + p, bi)),
            pl.BlockSpec((None, seq, LANES), lambda bi, p: (bi, 0, 0)),
        ],
        out_specs=pl.BlockSpec((STEP_WIDTH, seq), lambda bi, p: (p, bi)),
        out_shape=jax.ShapeDtypeStruct((GROUP_WIDTH, b * seq), jnp.bfloat16),
        scratch_shapes=[pltpu.VMEM((seq, LANES), jnp.bfloat16)] + _attn_scratch(seq),
        compiler_params=pltpu.CompilerParams(
            dimension_semantics=("arbitrary", "arbitrary"), vmem_limit_bytes=VMEM_LIMIT),
        name="fox_attn",
    )(qvt, k3, qvt, logf3)


def _dense_kernel(x_ref, ymt_ref, yft_ref, wo_ref, g_ref, wgu_ref, wd_ref, gfin_ref, o_ref, *, final):
    x1 = (x_ref[...]
          + lax.dot_general(ymt_ref[...], wo_ref[:GROUP_WIDTH, :], _TN,
                            preferred_element_type=jnp.float32)
          + lax.dot_general(yft_ref[...], wo_ref[GROUP_WIDTH:, :], _TN,
                            preferred_element_type=jnp.float32))
    h = _rms(x1, g_ref[...]).astype(jnp.bfloat16)
    acc = jnp.zeros_like(x1)
    for c in range(D_FF // FF_CHUNK):
        lo = c * FF_CHUNK
        gate = jnp.dot(h, wgu_ref[:, lo:lo + FF_CHUNK], preferred_element_type=jnp.float32)
        up = jnp.dot(h, wgu_ref[:, D_FF + lo:D_FF + lo + FF_CHUNK], preferred_element_type=jnp.float32)
        a = (gate / (1.0 + jnp.exp(-gate)) * up).astype(jnp.bfloat16)
        acc = acc + jnp.dot(a, wd_ref[lo:lo + FF_CHUNK, :], preferred_element_type=jnp.float32)
    acc = acc + x1
    if final:
        acc = _rms(acc, gfin_ref[...])
    o_ref[...] = acc


def _dense(x2d, ymt, yft, wo, g, wgu, wd, gfin, final):
    t = x2d.shape[0]
    const = lambda shape: pl.BlockSpec(shape, lambda i: (0,) * len(shape),
                                       pipeline_mode=pl.Buffered(1))
    return pl.pallas_call(
        functools.partial(_dense_kernel, final=final),
        grid=(t // TOKEN_TILE,),
        in_specs=[
            pl.BlockSpec((TOKEN_TILE, D_MODEL), lambda i: (i, 0)),
            pl.BlockSpec((GROUP_WIDTH, TOKEN_TILE), lambda i: (0, i)),
            pl.BlockSpec((GROUP_WIDTH, TOKEN_TILE), lambda i: (0, i)),
            const((D_MODEL, D_MODEL)),
            const((1, D_MODEL)),
            const((D_MODEL, 2 * D_FF)),
            const((D_FF, D_MODEL)),
            const((1, D_MODEL)),
        ],
        out_specs=pl.BlockSpec((TOKEN_TILE, D_MODEL), lambda i: (i, 0)),
        out_shape=jax.ShapeDtypeStruct((t, D_MODEL), jnp.float32),
        compiler_params=pltpu.CompilerParams(
            dimension_semantics=("arbitrary",), vmem_limit_bytes=VMEM_LIMIT),
        name="dense_ffn",
    )(x2d, ymt, yft, wo, g, wgu, wd, gfin)


def kernel(x, w_in, b_f, w_o, g_attn, w_gu, w_down, g_ffn, rel_bias, g_final):
    b, seq, d = x.shape
    depth = w_in.shape[0]
    assert d == D_MODEL and seq % BLOCK == 0 and (b * seq) % TOKEN_TILE == 0
    assert seq // BLOCK <= BF16_ROWS and C_PARTS * N_HEADS <= HEAD_DIM
    bf16 = jnp.bfloat16
    gw = GROUP_WIDTH
    bias = _bias_strips(rel_bias, seq)
    x2d = x.reshape(b * seq, d)
    for layer in range(depth):
        w = w_in[layer]
        w_k = jnp.concatenate([w[:, gw:2 * gw], w[:, 4 * gw:5 * gw]], axis=1).astype(bf16)
        w_qvt = jnp.concatenate([w[:, 0:gw], w[:, 3 * gw:4 * gw], w[:, 2 * gw:3 * gw],
                                 w[:, 5 * gw:6 * gw]], axis=1).T.astype(bf16)
        w_f = _forget_lane_layout(w[:, 6 * gw:]).astype(bf16)
        b_lanes = _forget_lane_layout(b_f[layer][None, :])
        k2, qvt, logf = _inproj(x2d, g_attn[layer][None, :], w_k, w_qvt, w_f, b_lanes)
        k3 = k2.reshape(b, seq, 2 * gw)
        ymt = _moba(k3, qvt, bias)
        yft = _fox(k3, qvt, logf.reshape(b, seq, LANES))
        x2d = _dense(x2d, ymt, yft, w_o[layer].astype(bf16), g_ffn[layer][None, :],
                     w_gu[layer].astype(bf16), w_down[layer].astype(bf16), g_final[None, :],
                     final=(layer == depth - 1))
    return x2d.reshape(b, seq, d)
```

```python
import functools
import math

import jax
import jax.numpy as jnp
from jax import lax
from jax.experimental import pallas as pl
from jax.experimental.pallas import tpu as pltpu

D_MODEL = 1024
HEAD_DIM = 64
N_HEADS = 8
GROUP_WIDTH = N_HEADS * HEAD_DIM
BLOCK = 256
TOPK = 3
NUM_BUCKETS = 32
MAX_DISTANCE = 1024
D_FF = 2816
RMS_EPS = 1e-6
SCALE = HEAD_DIM ** -0.5
NEG = -1e30

LANES = 128
SUBLANES = 8
BF16_ROWS = 16
HEADS_PER_PAIR = LANES // HEAD_DIM
PAIRS_PER_STEP = 2
HEADS_IN_STEP = PAIRS_PER_STEP * HEADS_PER_PAIR
STEP_WIDTH = PAIRS_PER_STEP * LANES
N_STEPS = GROUP_WIDTH // STEP_WIDTH
UNITS_PER_GROUP = 2 * HEADS_PER_PAIR
C_PARTS = 3
V_ROWS = HEAD_DIM + BF16_ROWS
SLAB = 16
FF_CHUNK = 256
TOKEN_TILE = 1024
VMEM_LIMIT = 56 * 1024 * 1024

_NT = (((1,), (1,)), ((), ()))
_TN = (((0,), (0,)), ((), ()))


def _rms(x, g):
    return x * lax.rsqrt(jnp.mean(x * x, axis=-1, keepdims=True) + RMS_EPS) * g


def _inproj_kernel(x_ref, g_ref, wk_ref, wqv_ref, wf_ref, bf_ref, k_ref, qvt_ref, logf_ref):
    h = _rms(x_ref[...], g_ref[...]).astype(jnp.bfloat16)
    k_ref[...] = jnp.dot(h, wk_ref[...], preferred_element_type=jnp.float32).astype(jnp.bfloat16)
    qvt_ref[...] = lax.dot_general(wqv_ref[...], h, _NT,
                                   preferred_element_type=jnp.float32).astype(jnp.bfloat16)
    z = jnp.dot(h, wf_ref[...], preferred_element_type=jnp.float32) + bf_ref[...]
    logf_ref[...] = -(jnp.maximum(-z, 0.0) + jnp.log1p(jnp.exp(-jnp.abs(z))))


def _inproj(x2d, g, wk, wqvt, wf, bf):
    t = x2d.shape[0]
    const = lambda shape: pl.BlockSpec(shape, lambda i: (0, 0))
    return pl.pallas_call(
        _inproj_kernel,
        grid=(t // TOKEN_TILE,),
        in_specs=[
            pl.BlockSpec((TOKEN_TILE, D_MODEL), lambda i: (i, 0)),
            const((1, D_MODEL)),
            const((D_MODEL, 2 * GROUP_WIDTH)),
            const((4 * GROUP_WIDTH, D_MODEL)),
            const((D_MODEL, LANES)),
            const((1, LANES)),
        ],
        out_specs=[
            pl.BlockSpec((TOKEN_TILE, 2 * GROUP_WIDTH), lambda i: (i, 0)),
            pl.BlockSpec((4 * GROUP_WIDTH, TOKEN_TILE), lambda i: (0, i)),
            pl.BlockSpec((TOKEN_TILE, LANES), lambda i: (i, 0)),
        ],
        out_shape=[
            jax.ShapeDtypeStruct((t, 2 * GROUP_WIDTH), jnp.bfloat16),
            jax.ShapeDtypeStruct((4 * GROUP_WIDTH, t), jnp.bfloat16),
            jax.ShapeDtypeStruct((t, LANES), jnp.float32),
        ],
        compiler_params=pltpu.CompilerParams(
            dimension_semantics=("arbitrary",), vmem_limit_bytes=VMEM_LIMIT),
        name="inproj",
    )(x2d, g, wk, wqvt, wf, bf)


def _forget_lane_layout(cols):
    rep = jnp.repeat(cols, C_PARTS, axis=-1)
    half = jnp.pad(rep, [(0, 0)] * (cols.ndim - 1) + [(0, HEAD_DIM - C_PARTS * N_HEADS)])
    return jnp.concatenate([half, half], axis=-1)


def _bias_kernel(relt_ref, bkt_ref, out_ref, *, seq):
    h = pl.program_id(0)
    bkt = bkt_ref[...]
    w = jnp.zeros(bkt.shape, jnp.float32)
    for b in range(NUM_BUCKETS):
        w = jnp.where(bkt == b, relt_ref[h, b], w)
    x = jnp.broadcast_to(w, (BLOCK, bkt.shape[1]))
    y = pltpu.roll(x, 0, 1, stride=1, stride_axis=0)
    out_ref[0] = y[:, BLOCK:].T


def _t5_bucket(dist):
    max_exact = NUM_BUCKETS // 2
    is_small = dist < max_exact
    d = jnp.maximum(dist, 1).astype(jnp.float32)
    large = max_exact + (jnp.log(d / max_exact) / math.log(MAX_DISTANCE / max_exact)
                         * (NUM_BUCKETS - max_exact)).astype(jnp.int32)
    large = jnp.minimum(large, NUM_BUCKETS - 1)
    return jnp.where(is_small, dist, large)


def _bias_strips(rel_bias, seq):
    n = jnp.arange(seq + BLOCK, dtype=jnp.int32)
    bkt = _t5_bucket(jnp.clip(seq - n, 0, seq - 1))[None, :]
    return pl.pallas_call(
        functools.partial(_bias_kernel, seq=seq),
        grid=(N_HEADS,),
        in_specs=[
            pl.BlockSpec(memory_space=pltpu.SMEM),
            pl.BlockSpec((1, seq + BLOCK), lambda h: (0, 0)),
        ],
        out_specs=pl.BlockSpec((1, seq, BLOCK), lambda h: (h, 0, 0)),
        out_shape=jax.ShapeDtypeStruct((N_HEADS, seq, BLOCK), jnp.float32),
        compiler_params=pltpu.CompilerParams(dimension_semantics=("arbitrary",)),
        name="moba_bias",
    )(rel_bias.T, bkt)


def _iota(shape, axis):
    return lax.broadcasted_iota(jnp.int32, shape, axis)


def _split3(x):
    p1 = x.astype(jnp.bfloat16)
    r1 = x - p1.astype(jnp.float32)
    p2 = r1.astype(jnp.bfloat16)
    p3 = (r1 - p2.astype(jnp.float32)).astype(jnp.bfloat16)
    return p1, p2, p3


def _augment_kv(k_ref, vt_ref, k_extra, kp_sc, vt_sc):
    seq = k_ref.shape[1]
    lane = _iota((seq, LANES), 1)
    for h in range(HEADS_IN_STEP):
        pr, hh = divmod(h, HEADS_PER_PAIR)
        hmask = (lane >= hh * HEAD_DIM) & (lane < (hh + 1) * HEAD_DIM)
        kp_sc[h] = jnp.where(hmask, k_ref[0, :, pr * LANES:(pr + 1) * LANES], k_extra)
        vt_sc[h, :HEAD_DIM, :] = vt_ref[h * HEAD_DIM:(h + 1) * HEAD_DIM, :]
        vt_sc[h, HEAD_DIM:, :] = jnp.ones((V_ROWS - HEAD_DIM, seq), vt_sc.dtype)


class _Unit:
    def __init__(self, h, i, qat, tile_term):
        self.h, self.i, self.qat, self.tile_term = h, i, qat, tile_term
        self.s_ref = self.mxp = self.m = self.acc = None


def _score_tile(u, j, kp_sc):
    key_in_tile = _iota((SLAB, BLOCK), 0)
    query_in_tile = _iota((SLAB, BLOCK), 1)
    s = jnp.dot(kp_sc[u.h, j * BLOCK:(j + 1) * BLOCK, :], u.qat, preferred_element_type=jnp.float32)
    for r in range(0, BLOCK, SLAB):
        piece = s[r:r + SLAB, :]
        if u.tile_term is not None:
            piece = piece + u.tile_term(j, r)
        if j == u.i:
            piece = jnp.where(key_in_tile + r <= query_in_tile, piece, NEG)
        u.s_ref[j * BLOCK + r:j * BLOCK + r + SLAB, :] = piece
        t = jnp.max(piece.reshape(SLAB // SUBLANES, SUBLANES, BLOCK), axis=0)
        u.mxp = t if u.mxp is None else jnp.maximum(u.mxp, t)


def _value_tile(u, j, vt_sc):
    if u.m is None:
        u.m = jnp.max(u.mxp, axis=0, keepdims=True)
    e = jnp.concatenate(
        [jnp.exp(u.s_ref[j * BLOCK + r:j * BLOCK + r + SLAB, :] - u.m).astype(jnp.bfloat16)
         for r in range(0, BLOCK, SLAB)], axis=0)
    pv = jnp.dot(vt_sc[u.h, :, j * BLOCK:(j + 1) * BLOCK], e, preferred_element_type=jnp.float32)
    u.acc = pv if u.acc is None else u.acc + pv


def _run_units(nb, make_unit, s_sc, kp_sc, vt_sc, o_ref):
    groups = [[(pr * HEADS_PER_PAIR + hh, i) for i in (a, nb - 1 - a) for hh in range(HEADS_PER_PAIR)]
              for pr in range(PAIRS_PER_STEP) for a in range(nb // 2)]
    prev = []
    for g, grp in enumerate(groups + [None]):
        cur = [make_unit(h, i) for h, i in grp] if grp is not None else []
        for n, u in enumerate(cur):
            u.s_ref = s_sc.at[(g % 2) * UNITS_PER_GROUP + n]
        for t in range(nb):
            for u in cur:
                if t <= u.i:
                    _score_tile(u, t, kp_sc)
            for u in prev:
                if t <= u.i:
                    _value_tile(u, t, vt_sc)
        for u in prev:
            out = u.acc[:HEAD_DIM, :] * (1.0 / u.acc[HEAD_DIM:HEAD_DIM + 1, :])
            o_ref[u.h * HEAD_DIM:(u.h + 1) * HEAD_DIM,
                  u.i * BLOCK:(u.i + 1) * BLOCK] = out.astype(o_ref.dtype)
        prev = cur


def _scaled_qt(qt_ref, h, i):
    pr = h // HEADS_PER_PAIR
    q = qt_ref[pr * LANES:(pr + 1) * LANES, i * BLOCK:(i + 1) * BLOCK]
    return q * jnp.asarray(SCALE, q.dtype)


def _with_spare_rows(qt, hh, spare):
    head = qt[hh * HEAD_DIM:(hh + 1) * HEAD_DIM, :]
    spare = jnp.concatenate(
        [spare, jnp.zeros((HEAD_DIM - spare.shape[0], BLOCK), qt.dtype)], axis=0)
    return jnp.concatenate([head, spare] if hh == 0 else [spare, head], axis=0)


def _attn_scratch(seq):
    return [pltpu.VMEM((2 * UNITS_PER_GROUP, seq, BLOCK), jnp.float32),
            pltpu.VMEM((HEADS_IN_STEP, seq, LANES), jnp.bfloat16),
            pltpu.VMEM((HEADS_IN_STEP, V_ROWS, seq), jnp.bfloat16)]


def _moba_kernel(qt_ref, k_ref, vt_ref, bias_ref, o_ref, s_sc, kp_sc, vt_sc, *, seq):
    nb = seq // BLOCK
    key_blk = _iota((seq, LANES), 0) // BLOCK
    onehot = ((_iota((seq, LANES), 1) % HEAD_DIM) == key_blk).astype(k_ref.dtype)
    _augment_kv(k_ref, vt_ref, onehot, kp_sc, vt_sc)
    blk_row = _iota((BF16_ROWS, BLOCK), 0)
    feat_row = _iota((LANES, BLOCK), 0)

    def pair_kmean(pr):
        k = k_ref[0, :, pr * LANES:(pr + 1) * LANES].astype(jnp.float32)
        kmean = jnp.sum(k.reshape(nb, BLOCK, LANES), axis=1) * (1.0 / BLOCK)
        return jnp.concatenate([kmean, jnp.zeros((BF16_ROWS - nb, LANES), jnp.float32)], axis=0)

    kmeans = [pair_kmean(pr) for pr in range(PAIRS_PER_STEP)]

    def make_unit(h, i):
        pr, hh = divmod(h, HEADS_PER_PAIR)
        qt = _scaled_qt(qt_ref, h, i)
        pen = jnp.zeros((BF16_ROWS, BLOCK), qt.dtype)
        if i > TOPK:
            in_head = (feat_row >= hh * HEAD_DIM) & (feat_row < (hh + 1) * HEAD_DIM)
            q_head = jnp.where(in_head, qt, jnp.zeros_like(qt)).astype(jnp.float32)
            gate = jnp.dot(kmeans[pr], q_head, precision=lax.Precision.HIGHEST,
                           preferred_element_type=jnp.float32)
            rank = jnp.zeros((BF16_ROWS, BLOCK), jnp.int32)
            for jp in range(i):
                other = gate[jp:jp + 1, :]
                beats = (other > gate) | ((other == gate) & (blk_row > jp))
                rank = rank + beats.astype(jnp.int32)
            pen = jnp.where((blk_row < i) & (rank >= TOPK), NEG, 0.0).astype(qt.dtype)

        def tile_term(j, r):
            off = (nb - 1 - i + j) * BLOCK + r
            return bias_ref[h, off:off + SLAB, :]

        return _Unit(h, i, _with_spare_rows(qt, hh, pen), tile_term)

    _run_units(nb, make_unit, s_sc, kp_sc, vt_sc, o_ref)


def _moba(k3, qvt, bias):
    b, seq, _ = k3.shape
    return pl.pallas_call(
        functools.partial(_moba_kernel, seq=seq),
        grid=(N_STEPS, b),
        in_specs=[
            pl.BlockSpec((STEP_WIDTH, seq), lambda p, bi: (p, bi)),
            pl.BlockSpec((1, seq, STEP_WIDTH), lambda p, bi: (bi, 0, p)),
            pl.BlockSpec((STEP_WIDTH, seq), lambda p, bi: (2 * N_STEPS + p, bi)),
            pl.BlockSpec((HEADS_IN_STEP, seq, BLOCK), lambda p, bi: (p, 0, 0),
                         pipeline_mode=pl.Buffered(1)),
        ],
        out_specs=pl.BlockSpec((STEP_WIDTH, seq), lambda p, bi: (p, bi)),
        out_shape=jax.ShapeDtypeStruct((GROUP_WIDTH, b * seq), jnp.bfloat16),
        scratch_shapes=_attn_scratch(seq),
        compiler_params=pltpu.CompilerParams(
            dimension_semantics=("arbitrary", "arbitrary"), vmem_limit_bytes=VMEM_LIMIT),
        name="moba_attn",
    )(qvt, k3, qvt, bias)


def _fox_kernel(qt_ref, k_ref, vt_ref, logf_ref, o_ref, ext_sc, s_sc, kp_sc, vt_sc, *, seq):
    nb = seq // BLOCK
    p = pl.program_id(1)

    @pl.when(p == 0)
    def _():
        tri = (_iota((BLOCK, BLOCK), 1) <= _iota((BLOCK, BLOCK), 0)).astype(jnp.bfloat16)
        lane = _iota((BLOCK, LANES), 1) % HEAD_DIM
        carry = jnp.zeros((1, LANES), jnp.float32)
        for ch in range(nb):
            parts = _split3(logf_ref[ch * BLOCK:(ch + 1) * BLOCK, :])
            cs = jnp.dot(tri, jnp.concatenate(parts, axis=1), preferred_element_type=jnp.float32)
            c = cs[:, :LANES] + cs[:, LANES:2 * LANES] + cs[:, 2 * LANES:] + carry
            carry = c[BLOCK - 1:BLOCK, :]
            p1, p2, p3 = _split3(-c)
            part = lane % C_PARTS
            ext = jnp.where(part == 0, p1, jnp.where(part == 1, p2, p3))
            ext_sc[ch * BLOCK:(ch + 1) * BLOCK, :] = jnp.where(
                lane < C_PARTS * N_HEADS, ext, jnp.zeros_like(ext))

    _augment_kv(k_ref, vt_ref, ext_sc[...], kp_sc, vt_sc)
    n_spare = -(-C_PARTS * N_HEADS // BF16_ROWS) * BF16_ROWS
    row = _iota((n_spare, BLOCK), 0)

    def make_unit(h, i):
        qt = _scaled_qt(qt_ref, h, i)
        first = C_PARTS * (HEADS_IN_STEP * p + h)
        spare = jnp.where((row >= first) & (row < first + C_PARTS), 1.0, 0.0).astype(qt.dtype)
        return _Unit(h, i, _with_spare_rows(qt, h % HEADS_PER_PAIR, spare), None)

    _run_units(nb, make_unit, s_sc, kp_sc, vt_sc, o_ref)


def _fox(k3, qvt, logf3):
    b, seq, _ = k3.shape
    return pl.pallas_call(
        functools.partial(_fox_kernel, seq=seq),
        grid=(b, N_STEPS),
        in_specs=[
            pl.BlockSpec((STEP_WIDTH, seq), lambda bi, p: (N_STEPS + p, bi)),
            pl.BlockSpec((1, seq, STEP_WIDTH), lambda bi, p: (bi, 0, N_STEPS + p)),
            pl.BlockSpec((STEP_WIDTH, seq), lambda bi, p: (3 * N_STEPS + p, bi)),
            pl.BlockSpec((None, seq, LANES), lambda bi, p: (bi, 0, 0)),
        ],
        out_specs=pl.BlockSpec((STEP_WIDTH, seq), lambda bi, p: (p, bi)),
        out_shape=jax.ShapeDtypeStruct((GROUP_WIDTH, b * seq), jnp.bfloat16),
        scratch_shapes=[pltpu.VMEM((seq, LANES), jnp.bfloat16)] + _attn_scratch(seq),
        compiler_params=pltpu.CompilerParams(
            dimension_semantics=("arbitrary", "arbitrary"), vmem_limit_bytes=VMEM_LIMIT),
        name="fox_attn",
    )(qvt, k3, qvt, logf3)


def _dense_kernel(x_ref, ymt_ref, yft_ref, wo_ref, g_ref, wgu_ref, wd_ref, gfin_ref, o_ref, *, final):
    x1 = (x_ref[...]
          + lax.dot_general(ymt_ref[...], wo_ref[:GROUP_WIDTH, :], _TN,
                            preferred_element_type=jnp.float32)
          + lax.dot_general(yft_ref[...], wo_ref[GROUP_WIDTH:, :], _TN,
                            preferred_element_type=jnp.float32))
    h = _rms(x1, g_ref[...]).astype(jnp.bfloat16)
    acc = jnp.zeros_like(x1)
    for c in range(D_FF // FF_CHUNK):
        lo = c * FF_CHUNK
        gate = jnp.dot(h, wgu_ref[:, lo:lo + FF_CHUNK], preferred_element_type=jnp.float32)
        up = jnp.dot(h, wgu_ref[:, D_FF + lo:D_FF + lo + FF_CHUNK], preferred_element_type=jnp.float32)
        a = (gate / (1.0 + jnp.exp(-gate)) * up).astype(jnp.bfloat16)
        acc = acc + jnp.dot(a, wd_ref[lo:lo + FF_CHUNK, :], preferred_element_type=jnp.float32)
    acc = acc + x1
    if final:
        acc = _rms(acc, gfin_ref[...])
    o_ref[...] = acc


def _dense(x2d, ymt, yft, wo, g, wgu, wd, gfin, final):
    t = x2d.shape[0]
    const = lambda shape: pl.BlockSpec(shape, lambda i: (0,) * len(shape),
                                       pipeline_mode=pl.Buffered(1))
    return pl.pallas_call(
        functools.partial(_dense_kernel, final=final),
        grid=(t // TOKEN_TILE,),
        in_specs=[
            pl.BlockSpec((TOKEN_TILE, D_MODEL), lambda i: (i, 0)),
            pl.BlockSpec((GROUP_WIDTH, TOKEN_TILE), lambda i: (0, i)),
            pl.BlockSpec((GROUP_WIDTH, TOKEN_TILE), lambda i: (0, i)),
            const((D_MODEL, D_MODEL)),
            const((1, D_MODEL)),
            const((D_MODEL, 2 * D_FF)),
            const((D_FF, D_MODEL)),
            const((1, D_MODEL)),
        ],
        out_specs=pl.BlockSpec((TOKEN_TILE, D_MODEL), lambda i: (i, 0)),
        out_shape=jax.ShapeDtypeStruct((t, D_MODEL), jnp.float32),
        compiler_params=pltpu.CompilerParams(
            dimension_semantics=("arbitrary",), vmem_limit_bytes=VMEM_LIMIT),
        name="dense_ffn",
    )(x2d, ymt, yft, wo, g, wgu, wd, gfin)


def kernel(x, w_in, b_f, w_o, g_attn, w_gu, w_down, g_ffn, rel_bias, g_final):
    b, seq, d = x.shape
    depth = w_in.shape[0]
    assert d == D_MODEL and seq % BLOCK == 0 and (b * seq) % TOKEN_TILE == 0
    assert seq // BLOCK <= BF16_ROWS and C_PARTS * N_HEADS <= HEAD_DIM
    bf16 = jnp.bfloat16
    gw = GROUP_WIDTH
    bias = _bias_strips(rel_bias, seq)
    x2d = x.reshape(b * seq, d)
    for layer in range(depth):
        w = w_in[layer]
        w_k = jnp.concatenate([w[:, gw:2 * gw], w[:, 4 * gw:5 * gw]], axis=1).astype(bf16)
        w_qvt = jnp.concatenate([w[:, 0:gw], w[:, 3 * gw:4 * gw], w[:, 2 * gw:3 * gw],
                                 w[:, 5 * gw:6 * gw]], axis=1).T.astype(bf16)
        w_f = _forget_lane_layout(w[:, 6 * gw:]).astype(bf16)
        b_lanes = _forget_lane_layout(b_f[layer][None, :])
        k2, qvt, logf = _inproj(x2d, g_attn[layer][None, :], w_k, w_qvt, w_f, b_lanes)
        k3 = k2.reshape(b, seq, 2 * gw)
        ymt = _moba(k3, qvt, bias)
        yft = _fox(k3, qvt, logf.reshape(b, seq, LANES))
        x2d = _dense(x2d, ymt, yft, w_o[layer].astype(bf16), g_ffn[layer][None, :],
                     w_gu[layer].astype(bf16), w_down[layer].astype(bf16), g_final[None, :],
                     final=(layer == depth - 1))
    return x2d.reshape(b, seq, d)
```

```python
import functools
import math

import jax
import jax.numpy as jnp
from jax import lax
from jax.experimental import pallas as pl
from jax.experimental.pallas import tpu as pltpu

D_MODEL = 1024
HEAD_DIM = 64
N_HEADS = 8
GROUP_WIDTH = N_HEADS * HEAD_DIM
BLOCK = 256
TOPK = 3
NUM_BUCKETS = 32
MAX_DISTANCE = 1024
D_FF = 2816
RMS_EPS = 1e-6
SCALE = HEAD_DIM ** -0.5
NEG = -1e30

LANES = 128
SUBLANES = 8
BF16_ROWS = 16
HEADS_PER_PAIR = LANES // HEAD_DIM
PAIRS_PER_STEP = 2
HEADS_IN_STEP = PAIRS_PER_STEP * HEADS_PER_PAIR
STEP_WIDTH = PAIRS_PER_STEP * LANES
N_STEPS = GROUP_WIDTH // STEP_WIDTH
UNITS_PER_GROUP = 2 * HEADS_PER_PAIR
C_PARTS = 3
V_ROWS = HEAD_DIM + BF16_ROWS
SLAB = 16
FF_CHUNK = 256
TOKEN_TILE = 1024
VMEM_LIMIT = 56 * 1024 * 1024

_NT = (((1,), (1,)), ((), ()))
_TN = (((0,), (0,)), ((), ()))


def _rms(x, g):
    return x * lax.rsqrt(jnp.mean(x * x, axis=-1, keepdims=True) + RMS_EPS) * g


def _inproj_kernel(x_ref, g_ref, w_ref, wf_ref, bf_ref, k_ref, qvt_ref, logf_ref):
    gw = GROUP_WIDTH
    h = _rms(x_ref[...], g_ref[...]).astype(jnp.bfloat16)

    def proj(group):
        return jnp.dot(h, w_ref[:, group * gw:(group + 1) * gw], preferred_element_type=jnp.float32)

    for n, group in enumerate((1, 4)):
        k_ref[:, n * gw:(n + 1) * gw] = proj(group).astype(jnp.bfloat16)
    for n, group in enumerate((0, 3, 2, 5)):
        qvt_ref[n * gw:(n + 1) * gw, :] = proj(group).T.astype(jnp.bfloat16)
    z = jnp.dot(h, wf_ref[...], preferred_element_type=jnp.float32) + bf_ref[...]
    logf_ref[...] = -(jnp.maximum(-z, 0.0) + jnp.log1p(jnp.exp(-jnp.abs(z))))


def _inproj(x2d, g, w, wf, bf):
    t = x2d.shape[0]
    const = lambda shape: pl.BlockSpec(shape, lambda i: (0, 0))
    return pl.pallas_call(
        _inproj_kernel,
        grid=(t // TOKEN_TILE,),
        in_specs=[
            pl.BlockSpec((TOKEN_TILE, D_MODEL), lambda i: (i, 0)),
            const((1, D_MODEL)),
            const((D_MODEL, 6 * GROUP_WIDTH)),
            const((D_MODEL, LANES)),
            const((1, LANES)),
        ],
        out_specs=[
            pl.BlockSpec((TOKEN_TILE, 2 * GROUP_WIDTH), lambda i: (i, 0)),
            pl.BlockSpec((4 * GROUP_WIDTH, TOKEN_TILE), lambda i: (0, i)),
            pl.BlockSpec((TOKEN_TILE, LANES), lambda i: (i, 0)),
        ],
        out_shape=[
            jax.ShapeDtypeStruct((t, 2 * GROUP_WIDTH), jnp.bfloat16),
            jax.ShapeDtypeStruct((4 * GROUP_WIDTH, t), jnp.bfloat16),
            jax.ShapeDtypeStruct((t, LANES), jnp.float32),
        ],
        compiler_params=pltpu.CompilerParams(
            dimension_semantics=("arbitrary",), vmem_limit_bytes=VMEM_LIMIT),
        name="inproj",
    )(x2d, g, w, wf, bf)


def _forget_lane_layout(cols):
    rep = jnp.repeat(cols, C_PARTS, axis=-1)
    half = jnp.pad(rep, [(0, 0)] * (cols.ndim - 1) + [(0, HEAD_DIM - C_PARTS * N_HEADS)])
    return jnp.concatenate([half, half], axis=-1)


def _bias_kernel(relt_ref, bkt_ref, out_ref, *, seq):
    h = pl.program_id(0)
    bkt = bkt_ref[...]
    w = jnp.zeros(bkt.shape, jnp.float32)
    for b in range(NUM_BUCKETS):
        w = jnp.where(bkt == b, relt_ref[h, b], w)
    x = jnp.broadcast_to(w, (BLOCK, bkt.shape[1]))
    y = pltpu.roll(x, 0, 1, stride=1, stride_axis=0)
    out_ref[0] = y[:, BLOCK:].T


def _t5_bucket(dist):
    max_exact = NUM_BUCKETS // 2
    is_small = dist < max_exact
    d = jnp.maximum(dist, 1).astype(jnp.float32)
    large = max_exact + (jnp.log(d / max_exact) / math.log(MAX_DISTANCE / max_exact)
                         * (NUM_BUCKETS - max_exact)).astype(jnp.int32)
    large = jnp.minimum(large, NUM_BUCKETS - 1)
    return jnp.where(is_small, dist, large)


def _bias_strips(rel_bias, seq):
    n = jnp.arange(seq + BLOCK, dtype=jnp.int32)
    bkt = _t5_bucket(jnp.clip(seq - n, 0, seq - 1))[None, :]
    return pl.pallas_call(
        functools.partial(_bias_kernel, seq=seq),
        grid=(N_HEADS,),
        in_specs=[
            pl.BlockSpec(memory_space=pltpu.SMEM),
            pl.BlockSpec((1, seq + BLOCK), lambda h: (0, 0)),
        ],
        out_specs=pl.BlockSpec((1, seq, BLOCK), lambda h: (h, 0, 0)),
        out_shape=jax.ShapeDtypeStruct((N_HEADS, seq, BLOCK), jnp.float32),
        compiler_params=pltpu.CompilerParams(dimension_semantics=("arbitrary",)),
        name="moba_bias",
    )(rel_bias.T, bkt)


def _iota(shape, axis):
    return lax.broadcasted_iota(jnp.int32, shape, axis)


def _split3(x):
    p1 = x.astype(jnp.bfloat16)
    r1 = x - p1.astype(jnp.float32)
    p2 = r1.astype(jnp.bfloat16)
    p3 = (r1 - p2.astype(jnp.float32)).astype(jnp.bfloat16)
    return p1, p2, p3


def _augment_kv(k_ref, vt_ref, k_extra, kp_sc, vt_sc):
    seq = k_ref.shape[1]
    lane = _iota((seq, LANES), 1)
    for h in range(HEADS_IN_STEP):
        pr, hh = divmod(h, HEADS_PER_PAIR)
        hmask = (lane >= hh * HEAD_DIM) & (lane < (hh + 1) * HEAD_DIM)
        kp_sc[h] = jnp.where(hmask, k_ref[0, :, pr * LANES:(pr + 1) * LANES], k_extra)
        vt_sc[h, :HEAD_DIM, :] = vt_ref[h * HEAD_DIM:(h + 1) * HEAD_DIM, :]
        vt_sc[h, HEAD_DIM:, :] = jnp.ones((V_ROWS - HEAD_DIM, seq), vt_sc.dtype)


class _Unit:
    def __init__(self, h, i, qat, tile_term):
        self.h, self.i, self.qat, self.tile_term = h, i, qat, tile_term
        self.s_ref = self.mxp = self.m = self.acc = None


def _score_tile(u, j, kp_sc):
    key_in_tile = _iota((SLAB, BLOCK), 0)
    query_in_tile = _iota((SLAB, BLOCK), 1)
    s = jnp.dot(kp_sc[u.h, j * BLOCK:(j + 1) * BLOCK, :], u.qat, preferred_element_type=jnp.float32)
    for r in range(0, BLOCK, SLAB):
        piece = s[r:r + SLAB, :]
        if u.tile_term is not None:
            piece = piece + u.tile_term(j, r)
        if j == u.i:
            piece = jnp.where(key_in_tile + r <= query_in_tile, piece, NEG)
        u.s_ref[j * BLOCK + r:j * BLOCK + r + SLAB, :] = piece
        t = jnp.max(piece.reshape(SLAB // SUBLANES, SUBLANES, BLOCK), axis=0)
        u.mxp = t if u.mxp is None else jnp.maximum(u.mxp, t)


def _value_tile(u, j, vt_sc):
    if u.m is None:
        u.m = jnp.max(u.mxp, axis=0, keepdims=True)
    e = jnp.concatenate(
        [jnp.exp(u.s_ref[j * BLOCK + r:j * BLOCK + r + SLAB, :] - u.m).astype(jnp.bfloat16)
         for r in range(0, BLOCK, SLAB)], axis=0)
    pv = jnp.dot(vt_sc[u.h, :, j * BLOCK:(j + 1) * BLOCK], e, preferred_element_type=jnp.float32)
    u.acc = pv if u.acc is None else u.acc + pv


def _run_units(nb, make_unit, s_sc, kp_sc, vt_sc, o_ref):
    groups = [[(pr * HEADS_PER_PAIR + hh, i) for i in (a, nb - 1 - a) for hh in range(HEADS_PER_PAIR)]
              for pr in range(PAIRS_PER_STEP) for a in range(nb // 2)]
    prev = []
    for g, grp in enumerate(groups + [None]):
        cur = [make_unit(h, i) for h, i in grp] if grp is not None else []
        for n, u in enumerate(cur):
            u.s_ref = s_sc.at[(g % 2) * UNITS_PER_GROUP + n]
        for t in range(nb):
            for u in cur:
                if t <= u.i:
                    _score_tile(u, t, kp_sc)
            for u in prev:
                if t <= u.i:
                    _value_tile(u, t, vt_sc)
        for u in prev:
            out = u.acc[:HEAD_DIM, :] * (1.0 / u.acc[HEAD_DIM:HEAD_DIM + 1, :])
            o_ref[u.h * HEAD_DIM:(u.h + 1) * HEAD_DIM,
                  u.i * BLOCK:(u.i + 1) * BLOCK] = out.astype(o_ref.dtype)
        prev = cur


def _scaled_qt(qt_ref, h, i):
    pr = h // HEADS_PER_PAIR
    q = qt_ref[pr * LANES:(pr + 1) * LANES, i * BLOCK:(i + 1) * BLOCK]
    return q * jnp.asarray(SCALE, q.dtype)


def _with_spare_rows(qt, hh, spare):
    head = qt[hh * HEAD_DIM:(hh + 1) * HEAD_DIM, :]
    spare = jnp.concatenate(
        [spare, jnp.zeros((HEAD_DIM - spare.shape[0], BLOCK), qt.dtype)], axis=0)
    return jnp.concatenate([head, spare] if hh == 0 else [spare, head], axis=0)


def _attn_scratch(seq):
    return [pltpu.VMEM((2 * UNITS_PER_GROUP, seq, BLOCK), jnp.float32),
            pltpu.VMEM((HEADS_IN_STEP, seq, LANES), jnp.bfloat16),
            pltpu.VMEM((HEADS_IN_STEP, V_ROWS, seq), jnp.bfloat16)]


def _moba_kernel(qt_ref, k_ref, vt_ref, bias_ref, o_ref, s_sc, kp_sc, vt_sc, *, seq):
    nb = seq // BLOCK
    key_blk = _iota((seq, LANES), 0) // BLOCK
    onehot = ((_iota((seq, LANES), 1) % HEAD_DIM) == key_blk).astype(k_ref.dtype)
    _augment_kv(k_ref, vt_ref, onehot, kp_sc, vt_sc)
    blk_row = _iota((BF16_ROWS, BLOCK), 0)
    feat_row = _iota((LANES, BLOCK), 0)

    def pair_kmean(pr):
        k = k_ref[0, :, pr * LANES:(pr + 1) * LANES].astype(jnp.float32)
        kmean = jnp.sum(k.reshape(nb, BLOCK, LANES), axis=1) * (1.0 / BLOCK)
        return jnp.concatenate([kmean, jnp.zeros((BF16_ROWS - nb, LANES), jnp.float32)], axis=0)

    kmeans = [pair_kmean(pr) for pr in range(PAIRS_PER_STEP)]

    def make_unit(h, i):
        pr, hh = divmod(h, HEADS_PER_PAIR)
        qt = _scaled_qt(qt_ref, h, i)
        pen = jnp.zeros((BF16_ROWS, BLOCK), qt.dtype)
        if i > TOPK:
            in_head = (feat_row >= hh * HEAD_DIM) & (feat_row < (hh + 1) * HEAD_DIM)
            q_head = jnp.where(in_head, qt, jnp.zeros_like(qt)).astype(jnp.float32)
            gate = jnp.dot(kmeans[pr], q_head, precision=lax.Precision.HIGHEST,
                           preferred_element_type=jnp.float32)
            rank = jnp.zeros((BF16_ROWS, BLOCK), jnp.int32)
            for jp in range(i):
                other = gate[jp:jp + 1, :]
                beats = (other > gate) | ((other == gate) & (blk_row > jp))
                rank = rank + beats.astype(jnp.int32)
            pen = jnp.where((blk_row < i) & (rank >= TOPK), NEG, 0.0).astype(qt.dtype)

        def tile_term(j, r):
            off = (nb - 1 - i + j) * BLOCK + r
            return bias_ref[h, off:off + SLAB, :]

        return _Unit(h, i, _with_spare_rows(qt, hh, pen), tile_term)

    _run_units(nb, make_unit, s_sc, kp_sc, vt_sc, o_ref)


def _moba(k3, qvt, bias):
    b, seq, _ = k3.shape
    return pl.pallas_call(
        functools.partial(_moba_kernel, seq=seq),
        grid=(N_STEPS, b),
        in_specs=[
            pl.BlockSpec((STEP_WIDTH, seq), lambda p, bi: (p, bi)),
            pl.BlockSpec((1, seq, STEP_WIDTH), lambda p, bi: (bi, 0, p)),
            pl.BlockSpec((STEP_WIDTH, seq), lambda p, bi: (2 * N_STEPS + p, bi)),
            pl.BlockSpec((HEADS_IN_STEP, seq, BLOCK), lambda p, bi: (p, 0, 0),
                         pipeline_mode=pl.Buffered(1)),
        ],
        out_specs=pl.BlockSpec((STEP_WIDTH, seq), lambda p, bi: (p, bi)),
        out_shape=jax.ShapeDtypeStruct((GROUP_WIDTH, b * seq), jnp.bfloat16),
        scratch_shapes=_attn_scratch(seq),
        compiler_params=pltpu.CompilerParams(
            dimension_semantics=("arbitrary", "arbitrary"), vmem_limit_bytes=VMEM_LIMIT),
        name="moba_attn",
    )(qvt, k3, qvt, bias)


def _fox_kernel(qt_ref, k_ref, vt_ref, logf_ref, o_ref, ext_sc, s_sc, kp_sc, vt_sc, *, seq):
    nb = seq // BLOCK
    p = pl.program_id(1)

    @pl.when(p == 0)
    def _():
        tri = (_iota((BLOCK, BLOCK), 1) <= _iota((BLOCK, BLOCK), 0)).astype(jnp.bfloat16)
        lane = _iota((BLOCK, LANES), 1) % HEAD_DIM
        carry = jnp.zeros((1, LANES), jnp.float32)
        for ch in range(nb):
            parts = _split3(logf_ref[ch * BLOCK:(ch + 1) * BLOCK, :])
            cs = jnp.dot(tri, jnp.concatenate(parts, axis=1), preferred_element_type=jnp.float32)
            c = cs[:, :LANES] + cs[:, LANES:2 * LANES] + cs[:, 2 * LANES:] + carry
            carry = c[BLOCK - 1:BLOCK, :]
            p1, p2, p3 = _split3(-c)
            part = lane % C_PARTS
            ext = jnp.where(part == 0, p1, jnp.where(part == 1, p2, p3))
            ext_sc[ch * BLOCK:(ch + 1) * BLOCK, :] = jnp.where(
                lane < C_PARTS * N_HEADS, ext, jnp.zeros_like(ext))

    _augment_kv(k_ref, vt_ref, ext_sc[...], kp_sc, vt_sc)
    n_spare = -(-C_PARTS * N_HEADS // BF16_ROWS) * BF16_ROWS
    row = _iota((n_spare, BLOCK), 0)

    def make_unit(h, i):
        qt = _scaled_qt(qt_ref, h, i)
        first = C_PARTS * (HEADS_IN_STEP * p + h)
        spare = jnp.where((row >= first) & (row < first + C_PARTS), 1.0, 0.0).astype(qt.dtype)
        return _Unit(h, i, _with_spare_rows(qt, h % HEADS_PER_PAIR, spare), None)

    _run_units(nb, make_unit, s_sc, kp_sc, vt_sc, o_ref)


def _fox(k3, qvt, logf3):
    b, seq, _ = k3.shape
    return pl.pallas_call(
        functools.partial(_fox_kernel, seq=seq),
        grid=(b, N_STEPS),
        in_specs=[
            pl.BlockSpec((STEP_WIDTH, seq), lambda bi, p: (N_STEPS + p, bi)),
            pl.BlockSpec((1, seq, STEP_WIDTH), lambda bi, p: (bi, 0, N_STEPS + p)),
            pl.BlockSpec((STEP_WIDTH, seq), lambda bi, p: (3 * N_STEPS + p, bi)),
            pl.BlockSpec((None, seq, LANES), lambda bi, p: (bi, 0, 0)),
        ],
        out_specs=pl.BlockSpec((STEP_WIDTH, seq), lambda bi, p: (p, bi)),
        out_shape=jax.ShapeDtypeStruct((GROUP_WIDTH, b * seq), jnp.bfloat16),
        scratch_shapes=[pltpu.VMEM((seq, LANES), jnp.bfloat16)] + _attn_scratch(seq),
        compiler_params=pltpu.CompilerParams(
            dimension_semantics=("arbitrary", "arbitrary"), vmem_limit_bytes=VMEM_LIMIT),
        name="fox_attn",
    )(qvt, k3, qvt, logf3)


def _dense_kernel(x_ref, ymt_ref, yft_ref, wo_ref, g_ref, wgu_ref, wd_ref, gfin_ref, o_ref, *, final):
    x1 = (x_ref[...]
          + lax.dot_general(ymt_ref[...], wo_ref[:GROUP_WIDTH, :], _TN,
                            preferred_element_type=jnp.float32)
          + lax.dot_general(yft_ref[...], wo_ref[GROUP_WIDTH:, :], _TN,
                            preferred_element_type=jnp.float32))
    h = _rms(x1, g_ref[...]).astype(jnp.bfloat16)
    acc = jnp.zeros_like(x1)
    for c in range(D_FF // FF_CHUNK):
        lo = c * FF_CHUNK
        gate = jnp.dot(h, wgu_ref[:, lo:lo + FF_CHUNK], preferred_element_type=jnp.float32)
        up = jnp.dot(h, wgu_ref[:, D_FF + lo:D_FF + lo + FF_CHUNK], preferred_element_type=jnp.float32)
        a = (gate / (1.0 + jnp.exp(-gate)) * up).astype(jnp.bfloat16)
        acc = acc + jnp.dot(a, wd_ref[lo:lo + FF_CHUNK, :], preferred_element_type=jnp.float32)
    acc = acc + x1
    if final:
        acc = _rms(acc, gfin_ref[...])
    o_ref[...] = acc


def _dense(x2d, ymt, yft, wo, g, wgu, wd, gfin, final):
    t = x2d.shape[0]
    const = lambda shape: pl.BlockSpec(shape, lambda i: (0,) * len(shape),
                                       pipeline_mode=pl.Buffered(1))
    return pl.pallas_call(
        functools.partial(_dense_kernel, final=final),
        grid=(t // TOKEN_TILE,),
        in_specs=[
            pl.BlockSpec((TOKEN_TILE, D_MODEL), lambda i: (i, 0)),
            pl.BlockSpec((GROUP_WIDTH, TOKEN_TILE), lambda i: (0, i)),
            pl.BlockSpec((GROUP_WIDTH, TOKEN_TILE), lambda i: (0, i)),
            const((D_MODEL, D_MODEL)),
            const((1, D_MODEL)),
            const((D_MODEL, 2 * D_FF)),
            const((D_FF, D_MODEL)),
            const((1, D_MODEL)),
        ],
        out_specs=pl.BlockSpec((TOKEN_TILE, D_MODEL), lambda i: (i, 0)),
        out_shape=jax.ShapeDtypeStruct((t, D_MODEL), jnp.float32),
        compiler_params=pltpu.CompilerParams(
            dimension_semantics=("arbitrary",), vmem_limit_bytes=VMEM_LIMIT),
        name="dense_ffn",
    )(x2d, ymt, yft, wo, g, wgu, wd, gfin)


def kernel(x, w_in, b_f, w_o, g_attn, w_gu, w_down, g_ffn, rel_bias, g_final):
    b, seq, d = x.shape
    depth = w_in.shape[0]
    assert d == D_MODEL and seq % BLOCK == 0 and (b * seq) % TOKEN_TILE == 0
    assert seq // BLOCK <= BF16_ROWS and C_PARTS * N_HEADS <= HEAD_DIM
    bf16 = jnp.bfloat16
    gw = GROUP_WIDTH
    bias = _bias_strips(rel_bias, seq)
    x2d = x.reshape(b * seq, d)
    for layer in range(depth):
        w_f = _forget_lane_layout(w_in[layer, :, 6 * gw:]).astype(bf16)
        b_lanes = _forget_lane_layout(b_f[layer][None, :])
        k2, qvt, logf = _inproj(x2d, g_attn[layer][None, :], w_in[layer, :, :6 * gw].astype(bf16),
                                w_f, b_lanes)
        k3 = k2.reshape(b, seq, 2 * gw)
        ymt = _moba(k3, qvt, bias)
        yft = _fox(k3, qvt, logf.reshape(b, seq, LANES))
        x2d = _dense(x2d, ymt, yft, w_o[layer].astype(bf16), g_ffn[layer][None, :],
                     w_gu[layer].astype(bf16), w_down[layer].astype(bf16), g_final[None, :],
                     final=(layer == depth - 1))
    return x2d.reshape(b, seq, d)
```

```python
import functools
import math

import jax
import jax.numpy as jnp
from jax import lax
from jax.experimental import pallas as pl
from jax.experimental.pallas import tpu as pltpu

D_MODEL = 1024
HEAD_DIM = 64
N_HEADS = 8
GROUP_WIDTH = N_HEADS * HEAD_DIM
BLOCK = 256
TOPK = 3
NUM_BUCKETS = 32
MAX_DISTANCE = 1024
D_FF = 2816
RMS_EPS = 1e-6
SCALE = HEAD_DIM ** -0.5
NEG = -1e30

LANES = 128
SUBLANES = 8
BF16_ROWS = 16
HEADS_PER_PAIR = LANES // HEAD_DIM
PAIRS_PER_STEP = 2
HEADS_IN_STEP = PAIRS_PER_STEP * HEADS_PER_PAIR
STEP_WIDTH = PAIRS_PER_STEP * LANES
N_STEPS = GROUP_WIDTH // STEP_WIDTH
UNITS_PER_GROUP = 2 * HEADS_PER_PAIR
C_PARTS = 3
V_ROWS = HEAD_DIM + BF16_ROWS
SLAB = 16
FF_CHUNK = 256
TOKEN_TILE = 1024
VMEM_LIMIT = 56 * 1024 * 1024

_NT = (((1,), (1,)), ((), ()))
_TN = (((0,), (0,)), ((), ()))


def _rms(x, g):
    return x * lax.rsqrt(jnp.mean(x * x, axis=-1, keepdims=True) + RMS_EPS) * g


def _inproj_kernel(x_ref, g_ref, w_ref, wf_ref, bf_ref, k_ref, qvt_ref, logf_ref):
    gw = GROUP_WIDTH
    h = _rms(x_ref[...], g_ref[...]).astype(jnp.bfloat16)

    def proj(group):
        return jnp.dot(h, w_ref[:, group * gw:(group + 1) * gw], preferred_element_type=jnp.float32)

    for n, group in enumerate((1, 4)):
        k_ref[:, n * gw:(n + 1) * gw] = proj(group).astype(jnp.bfloat16)
    for n, group in enumerate((0, 3, 2, 5)):
        qvt_ref[n * gw:(n + 1) * gw, :] = proj(group).T.astype(jnp.bfloat16)
    z = jnp.dot(h, wf_ref[...], preferred_element_type=jnp.float32) + bf_ref[...]
    logf_ref[...] = -(jnp.maximum(-z, 0.0) + jnp.log1p(jnp.exp(-jnp.abs(z))))


def _layer_block(stacked, layer, **kwargs):
    return pl.BlockSpec((None,) + stacked.shape[1:], lambda i: (layer, 0, 0), **kwargs)


def _inproj(x2d, g, w_all, layer, wf, bf):
    t = x2d.shape[0]
    const = lambda shape: pl.BlockSpec(shape, lambda i: (0, 0))
    return pl.pallas_call(
        _inproj_kernel,
        grid=(t // TOKEN_TILE,),
        in_specs=[
            pl.BlockSpec((TOKEN_TILE, D_MODEL), lambda i: (i, 0)),
            const((1, D_MODEL)),
            _layer_block(w_all, layer),
            const((D_MODEL, LANES)),
            const((1, LANES)),
        ],
        out_specs=[
            pl.BlockSpec((TOKEN_TILE, 2 * GROUP_WIDTH), lambda i: (i, 0)),
            pl.BlockSpec((4 * GROUP_WIDTH, TOKEN_TILE), lambda i: (0, i)),
            pl.BlockSpec((TOKEN_TILE, LANES), lambda i: (i, 0)),
        ],
        out_shape=[
            jax.ShapeDtypeStruct((t, 2 * GROUP_WIDTH), jnp.bfloat16),
            jax.ShapeDtypeStruct((4 * GROUP_WIDTH, t), jnp.bfloat16),
            jax.ShapeDtypeStruct((t, LANES), jnp.float32),
        ],
        compiler_params=pltpu.CompilerParams(
            dimension_semantics=("arbitrary",), vmem_limit_bytes=VMEM_LIMIT),
        name="inproj",
    )(x2d, g, w_all, wf, bf)


def _forget_lane_layout(cols):
    rep = jnp.repeat(cols, C_PARTS, axis=-1)
    half = jnp.pad(rep, [(0, 0)] * (cols.ndim - 1) + [(0, HEAD_DIM - C_PARTS * N_HEADS)])
    return jnp.concatenate([half, half], axis=-1)


def _bias_kernel(relt_ref, bkt_ref, out_ref, *, seq):
    h = pl.program_id(0)
    bkt = bkt_ref[...]
    w = jnp.zeros(bkt.shape, jnp.float32)
    for b in range(NUM_BUCKETS):
        w = jnp.where(bkt == b, relt_ref[h, b], w)
    x = jnp.broadcast_to(w, (BLOCK, bkt.shape[1]))
    y = pltpu.roll(x, 0, 1, stride=1, stride_axis=0)
    out_ref[0] = y[:, BLOCK:].T


def _t5_bucket(dist):
    max_exact = NUM_BUCKETS // 2
    is_small = dist < max_exact
    d = jnp.maximum(dist, 1).astype(jnp.float32)
    large = max_exact + (jnp.log(d / max_exact) / math.log(MAX_DISTANCE / max_exact)
                         * (NUM_BUCKETS - max_exact)).astype(jnp.int32)
    large = jnp.minimum(large, NUM_BUCKETS - 1)
    return jnp.where(is_small, dist, large)


def _bias_strips(rel_bias, seq):
    n = jnp.arange(seq + BLOCK, dtype=jnp.int32)
    bkt = _t5_bucket(jnp.clip(seq - n, 0, seq - 1))[None, :]
    return pl.pallas_call(
        functools.partial(_bias_kernel, seq=seq),
        grid=(N_HEADS,),
        in_specs=[
            pl.BlockSpec(memory_space=pltpu.SMEM),
            pl.BlockSpec((1, seq + BLOCK), lambda h: (0, 0)),
        ],
        out_specs=pl.BlockSpec((1, seq, BLOCK), lambda h: (h, 0, 0)),
        out_shape=jax.ShapeDtypeStruct((N_HEADS, seq, BLOCK), jnp.float32),
        compiler_params=pltpu.CompilerParams(dimension_semantics=("arbitrary",)),
        name="moba_bias",
    )(rel_bias.T, bkt)


def _iota(shape, axis):
    return lax.broadcasted_iota(jnp.int32, shape, axis)


def _split3(x):
    p1 = x.astype(jnp.bfloat16)
    r1 = x - p1.astype(jnp.float32)
    p2 = r1.astype(jnp.bfloat16)
    p3 = (r1 - p2.astype(jnp.float32)).astype(jnp.bfloat16)
    return p1, p2, p3


def _augment_kv(k_ref, vt_ref, k_extra, kp_sc, vt_sc):
    seq = k_ref.shape[1]
    lane = _iota((seq, LANES), 1)
    for h in range(HEADS_IN_STEP):
        pr, hh = divmod(h, HEADS_PER_PAIR)
        hmask = (lane >= hh * HEAD_DIM) & (lane < (hh + 1) * HEAD_DIM)
        kp_sc[h] = jnp.where(hmask, k_ref[0, :, pr * LANES:(pr + 1) * LANES], k_extra)
        vt_sc[h, :HEAD_DIM, :] = vt_ref[h * HEAD_DIM:(h + 1) * HEAD_DIM, :]
        vt_sc[h, HEAD_DIM:, :] = jnp.ones((V_ROWS - HEAD_DIM, seq), vt_sc.dtype)


class _Unit:
    def __init__(self, h, i, qat, tile_term):
        self.h, self.i, self.qat, self.tile_term = h, i, qat, tile_term
        self.s_ref = self.mxp = self.m = self.acc = None


def _score_tile(u, j, kp_sc):
    key_in_tile = _iota((SLAB, BLOCK), 0)
    query_in_tile = _iota((SLAB, BLOCK), 1)
    s = jnp.dot(kp_sc[u.h, j * BLOCK:(j + 1) * BLOCK, :], u.qat, preferred_element_type=jnp.float32)
    for r in range(0, BLOCK, SLAB):
        piece = s[r:r + SLAB, :]
        if u.tile_term is not None:
            piece = piece + u.tile_term(j, r)
        if j == u.i:
            piece = jnp.where(key_in_tile + r <= query_in_tile, piece, NEG)
        u.s_ref[j * BLOCK + r:j * BLOCK + r + SLAB, :] = piece
        t = jnp.max(piece.reshape(SLAB // SUBLANES, SUBLANES, BLOCK), axis=0)
        u.mxp = t if u.mxp is None else jnp.maximum(u.mxp, t)


def _value_tile(u, j, vt_sc):
    if u.m is None:
        u.m = jnp.max(u.mxp, axis=0, keepdims=True)
    e = jnp.concatenate(
        [jnp.exp(u.s_ref[j * BLOCK + r:j * BLOCK + r + SLAB, :] - u.m).astype(jnp.bfloat16)
         for r in range(0, BLOCK, SLAB)], axis=0)
    pv = jnp.dot(vt_sc[u.h, :, j * BLOCK:(j + 1) * BLOCK], e, preferred_element_type=jnp.float32)
    u.acc = pv if u.acc is None else u.acc + pv


def _run_units(nb, make_unit, s_sc, kp_sc, vt_sc, o_ref):
    groups = [[(pr * HEADS_PER_PAIR + hh, i) for i in (a, nb - 1 - a) for hh in range(HEADS_PER_PAIR)]
              for pr in range(PAIRS_PER_STEP) for a in range(nb // 2)]
    prev = []
    for g, grp in enumerate(groups + [None]):
        cur = [make_unit(h, i) for h, i in grp] if grp is not None else []
        for n, u in enumerate(cur):
            u.s_ref = s_sc.at[(g % 2) * UNITS_PER_GROUP + n]
        for t in range(nb):
            for u in cur:
                if t <= u.i:
                    _score_tile(u, t, kp_sc)
            for u in prev:
                if t <= u.i:
                    _value_tile(u, t, vt_sc)
        for u in prev:
            out = u.acc[:HEAD_DIM, :] * (1.0 / u.acc[HEAD_DIM:HEAD_DIM + 1, :])
            o_ref[u.h * HEAD_DIM:(u.h + 1) * HEAD_DIM,
                  u.i * BLOCK:(u.i + 1) * BLOCK] = out.astype(o_ref.dtype)
        prev = cur


def _scaled_qt(qt_ref, h, i):
    pr = h // HEADS_PER_PAIR
    q = qt_ref[pr * LANES:(pr + 1) * LANES, i * BLOCK:(i + 1) * BLOCK]
    return q * jnp.asarray(SCALE, q.dtype)


def _with_spare_rows(qt, hh, spare):
    head = qt[hh * HEAD_DIM:(hh + 1) * HEAD_DIM, :]
    spare = jnp.concatenate(
        [spare, jnp.zeros((HEAD_DIM - spare.shape[0], BLOCK), qt.dtype)], axis=0)
    return jnp.concatenate([head, spare] if hh == 0 else [spare, head], axis=0)


def _attn_scratch(seq):
    return [pltpu.VMEM((2 * UNITS_PER_GROUP, seq, BLOCK), jnp.float32),
            pltpu.VMEM((HEADS_IN_STEP, seq, LANES), jnp.bfloat16),
            pltpu.VMEM((HEADS_IN_STEP, V_ROWS, seq), jnp.bfloat16)]


def _moba_kernel(qt_ref, k_ref, vt_ref, bias_ref, o_ref, s_sc, kp_sc, vt_sc, *, seq):
    nb = seq // BLOCK
    key_blk = _iota((seq, LANES), 0) // BLOCK
    onehot = ((_iota((seq, LANES), 1) % HEAD_DIM) == key_blk).astype(k_ref.dtype)
    _augment_kv(k_ref, vt_ref, onehot, kp_sc, vt_sc)
    blk_row = _iota((BF16_ROWS, BLOCK), 0)
    feat_row = _iota((LANES, BLOCK), 0)

    def pair_kmean(pr):
        k = k_ref[0, :, pr * LANES:(pr + 1) * LANES].astype(jnp.float32)
        kmean = jnp.sum(k.reshape(nb, BLOCK, LANES), axis=1) * (1.0 / BLOCK)
        return jnp.concatenate([kmean, jnp.zeros((BF16_ROWS - nb, LANES), jnp.float32)], axis=0)

    kmeans = [pair_kmean(pr) for pr in range(PAIRS_PER_STEP)]

    def make_unit(h, i):
        pr, hh = divmod(h, HEADS_PER_PAIR)
        qt = _scaled_qt(qt_ref, h, i)
        pen = jnp.zeros((BF16_ROWS, BLOCK), qt.dtype)
        if i > TOPK:
            in_head = (feat_row >= hh * HEAD_DIM) & (feat_row < (hh + 1) * HEAD_DIM)
            q_head = jnp.where(in_head, qt, jnp.zeros_like(qt)).astype(jnp.float32)
            gate = jnp.dot(kmeans[pr], q_head, precision=lax.Precision.HIGHEST,
                           preferred_element_type=jnp.float32)
            rank = jnp.zeros((BF16_ROWS, BLOCK), jnp.int32)
            for jp in range(i):
                other = gate[jp:jp + 1, :]
                beats = (other > gate) | ((other == gate) & (blk_row > jp))
                rank = rank + beats.astype(jnp.int32)
            pen = jnp.where((blk_row < i) & (rank >= TOPK), NEG, 0.0).astype(qt.dtype)

        def tile_term(j, r):
            off = (nb - 1 - i + j) * BLOCK + r
            return bias_ref[h, off:off + SLAB, :]

        return _Unit(h, i, _with_spare_rows(qt, hh, pen), tile_term)

    _run_units(nb, make_unit, s_sc, kp_sc, vt_sc, o_ref)


def _moba(k3, qvt, bias):
    b, seq, _ = k3.shape
    return pl.pallas_call(
        functools.partial(_moba_kernel, seq=seq),
        grid=(N_STEPS, b),
        in_specs=[
            pl.BlockSpec((STEP_WIDTH, seq), lambda p, bi: (p, bi)),
            pl.BlockSpec((1, seq, STEP_WIDTH), lambda p, bi: (bi, 0, p)),
            pl.BlockSpec((STEP_WIDTH, seq), lambda p, bi: (2 * N_STEPS + p, bi)),
            pl.BlockSpec((HEADS_IN_STEP, seq, BLOCK), lambda p, bi: (p, 0, 0),
                         pipeline_mode=pl.Buffered(1)),
        ],
        out_specs=pl.BlockSpec((STEP_WIDTH, seq), lambda p, bi: (p, bi)),
        out_shape=jax.ShapeDtypeStruct((GROUP_WIDTH, b * seq), jnp.bfloat16),
        scratch_shapes=_attn_scratch(seq),
        compiler_params=pltpu.CompilerParams(
            dimension_semantics=("arbitrary", "arbitrary"), vmem_limit_bytes=VMEM_LIMIT),
        name="moba_attn",
    )(qvt, k3, qvt, bias)


def _fox_kernel(qt_ref, k_ref, vt_ref, logf_ref, o_ref, ext_sc, s_sc, kp_sc, vt_sc, *, seq):
    nb = seq // BLOCK
    p = pl.program_id(1)

    @pl.when(p == 0)
    def _():
        tri = (_iota((BLOCK, BLOCK), 1) <= _iota((BLOCK, BLOCK), 0)).astype(jnp.bfloat16)
        lane = _iota((BLOCK, LANES), 1) % HEAD_DIM
        carry = jnp.zeros((1, LANES), jnp.float32)
        for ch in range(nb):
            parts = _split3(logf_ref[ch * BLOCK:(ch + 1) * BLOCK, :])
            cs = jnp.dot(tri, jnp.concatenate(parts, axis=1), preferred_element_type=jnp.float32)
            c = cs[:, :LANES] + cs[:, LANES:2 * LANES] + cs[:, 2 * LANES:] + carry
            carry = c[BLOCK - 1:BLOCK, :]
            p1, p2, p3 = _split3(-c)
            part = lane % C_PARTS
            ext = jnp.where(part == 0, p1, jnp.where(part == 1, p2, p3))
            ext_sc[ch * BLOCK:(ch + 1) * BLOCK, :] = jnp.where(
                lane < C_PARTS * N_HEADS, ext, jnp.zeros_like(ext))

    _augment_kv(k_ref, vt_ref, ext_sc[...], kp_sc, vt_sc)
    n_spare = -(-C_PARTS * N_HEADS // BF16_ROWS) * BF16_ROWS
    row = _iota((n_spare, BLOCK), 0)

    def make_unit(h, i):
        qt = _scaled_qt(qt_ref, h, i)
        first = C_PARTS * (HEADS_IN_STEP * p + h)
        spare = jnp.where((row >= first) & (row < first + C_PARTS), 1.0, 0.0).astype(qt.dtype)
        return _Unit(h, i, _with_spare_rows(qt, h % HEADS_PER_PAIR, spare), None)

    _run_units(nb, make_unit, s_sc, kp_sc, vt_sc, o_ref)


def _fox(k3, qvt, logf3):
    b, seq, _ = k3.shape
    return pl.pallas_call(
        functools.partial(_fox_kernel, seq=seq),
        grid=(b, N_STEPS),
        in_specs=[
            pl.BlockSpec((STEP_WIDTH, seq), lambda bi, p: (N_STEPS + p, bi)),
            pl.BlockSpec((1, seq, STEP_WIDTH), lambda bi, p: (bi, 0, N_STEPS + p)),
            pl.BlockSpec((STEP_WIDTH, seq), lambda bi, p: (3 * N_STEPS + p, bi)),
            pl.BlockSpec((None, seq, LANES), lambda bi, p: (bi, 0, 0)),
        ],
        out_specs=pl.BlockSpec((STEP_WIDTH, seq), lambda bi, p: (p, bi)),
        out_shape=jax.ShapeDtypeStruct((GROUP_WIDTH, b * seq), jnp.bfloat16),
        scratch_shapes=[pltpu.VMEM((seq, LANES), jnp.bfloat16)] + _attn_scratch(seq),
        compiler_params=pltpu.CompilerParams(
            dimension_semantics=("arbitrary", "arbitrary"), vmem_limit_bytes=VMEM_LIMIT),
        name="fox_attn",
    )(qvt, k3, qvt, logf3)


def _dense_kernel(x_ref, ymt_ref, yft_ref, wo_ref, g_ref, wgu_ref, wd_ref, gfin_ref, o_ref, *, final):
    x1 = (x_ref[...]
          + lax.dot_general(ymt_ref[...], wo_ref[:GROUP_WIDTH, :], _TN,
                            preferred_element_type=jnp.float32)
          + lax.dot_general(yft_ref[...], wo_ref[GROUP_WIDTH:, :], _TN,
                            preferred_element_type=jnp.float32))
    h = _rms(x1, g_ref[...]).astype(jnp.bfloat16)
    acc = jnp.zeros_like(x1)
    for c in range(D_FF // FF_CHUNK):
        lo = c * FF_CHUNK
        gate = jnp.dot(h, wgu_ref[:, lo:lo + FF_CHUNK], preferred_element_type=jnp.float32)
        up = jnp.dot(h, wgu_ref[:, D_FF + lo:D_FF + lo + FF_CHUNK], preferred_element_type=jnp.float32)
        a = (gate / (1.0 + jnp.exp(-gate)) * up).astype(jnp.bfloat16)
        acc = acc + jnp.dot(a, wd_ref[lo:lo + FF_CHUNK, :], preferred_element_type=jnp.float32)
    acc = acc + x1
    if final:
        acc = _rms(acc, gfin_ref[...])
    o_ref[...] = acc


def _dense(x2d, ymt, yft, wo_all, g, wgu_all, wd_all, gfin, layer, final):
    t = x2d.shape[0]
    const = lambda shape: pl.BlockSpec(shape, lambda i: (0,) * len(shape),
                                       pipeline_mode=pl.Buffered(1))
    weights = lambda stacked: _layer_block(stacked, layer, pipeline_mode=pl.Buffered(1))
    return pl.pallas_call(
        functools.partial(_dense_kernel, final=final),
        grid=(t // TOKEN_TILE,),
        in_specs=[
            pl.BlockSpec((TOKEN_TILE, D_MODEL), lambda i: (i, 0)),
            pl.BlockSpec((GROUP_WIDTH, TOKEN_TILE), lambda i: (0, i)),
            pl.BlockSpec((GROUP_WIDTH, TOKEN_TILE), lambda i: (0, i)),
            weights(wo_all),
            const((1, D_MODEL)),
            weights(wgu_all),
            weights(wd_all),
            const((1, D_MODEL)),
        ],
        out_specs=pl.BlockSpec((TOKEN_TILE, D_MODEL), lambda i: (i, 0)),
        out_shape=jax.ShapeDtypeStruct((t, D_MODEL), jnp.float32),
        compiler_params=pltpu.CompilerParams(
            dimension_semantics=("arbitrary",), vmem_limit_bytes=VMEM_LIMIT),
        name="dense_ffn",
    )(x2d, ymt, yft, wo_all, g, wgu_all, wd_all, gfin)


def kernel(x, w_in, b_f, w_o, g_attn, w_gu, w_down, g_ffn, rel_bias, g_final):
    b, seq, d = x.shape
    depth = w_in.shape[0]
    assert d == D_MODEL and seq % BLOCK == 0 and (b * seq) % TOKEN_TILE == 0
    assert seq // BLOCK <= BF16_ROWS and C_PARTS * N_HEADS <= HEAD_DIM
    bf16 = jnp.bfloat16
    gw = GROUP_WIDTH
    bias = _bias_strips(rel_bias, seq)
    w_in16, w_o16, w_gu16, w_down16 = (w.astype(bf16) for w in (w_in, w_o, w_gu, w_down))
    w_f = _forget_lane_layout(w_in[:, :, 6 * gw:]).astype(bf16)
    b_lanes = _forget_lane_layout(b_f[:, None, :])
    x2d = x.reshape(b * seq, d)
    for layer in range(depth):
        k2, qvt, logf = _inproj(x2d, g_attn[layer][None, :], w_in16, layer, w_f[layer], b_lanes[layer])
        k3 = k2.reshape(b, seq, 2 * gw)
        ymt = _moba(k3, qvt, bias)
        yft = _fox(k3, qvt, logf.reshape(b, seq, LANES))
        x2d = _dense(x2d, ymt, yft, w_o16, g_ffn[layer][None, :], w_gu16, w_down16,
                     g_final[None, :], layer, final=(layer == depth - 1))
    return x2d.reshape(b, seq, d)
```

```python
import functools
import math

import jax
import jax.numpy as jnp
from jax import lax
from jax.experimental import pallas as pl
from jax.experimental.pallas import tpu as pltpu

D_MODEL = 1024
HEAD_DIM = 64
N_HEADS = 8
GROUP_WIDTH = N_HEADS * HEAD_DIM
BLOCK = 256
TOPK = 3
NUM_BUCKETS = 32
MAX_DISTANCE = 1024
D_FF = 2816
RMS_EPS = 1e-6
SCALE = HEAD_DIM ** -0.5
NEG = -1e30

LANES = 128
SUBLANES = 8
BF16_ROWS = 16
HEADS_PER_PAIR = LANES // HEAD_DIM
PAIRS_PER_STEP = 2
HEADS_IN_STEP = PAIRS_PER_STEP * HEADS_PER_PAIR
STEP_WIDTH = PAIRS_PER_STEP * LANES
N_STEPS = GROUP_WIDTH // STEP_WIDTH
UNITS_PER_GROUP = 2 * HEADS_PER_PAIR
C_PARTS = 3
V_ROWS = HEAD_DIM + BF16_ROWS
SLAB = 16
FF_CHUNK = 256
TOKEN_TILE = 1024
VMEM_LIMIT = 56 * 1024 * 1024

_NT = (((1,), (1,)), ((), ()))
_TN = (((0,), (0,)), ((), ()))


def _rms(x, g):
    return x * lax.rsqrt(jnp.mean(x * x, axis=-1, keepdims=True) + RMS_EPS) * g


def _inproj_kernel(x_ref, g_ref, w_ref, wf_ref, bf_ref, k_ref, qvt_ref, logf_ref):
    gw = GROUP_WIDTH
    h = _rms(x_ref[...], g_ref[...]).astype(jnp.bfloat16)

    def proj(group):
        w = w_ref[:, group * gw:(group + 1) * gw].astype(jnp.bfloat16)
        return jnp.dot(h, w, preferred_element_type=jnp.float32)

    for n, group in enumerate((1, 4)):
        k_ref[:, n * gw:(n + 1) * gw] = proj(group).astype(jnp.bfloat16)
    for n, group in enumerate((0, 3, 2, 5)):
        qvt_ref[n * gw:(n + 1) * gw, :] = proj(group).T.astype(jnp.bfloat16)
    z = jnp.dot(h, wf_ref[...], preferred_element_type=jnp.float32) + bf_ref[...]
    logf_ref[...] = -(jnp.maximum(-z, 0.0) + jnp.log1p(jnp.exp(-jnp.abs(z))))


def _layer_block(stacked, layer, **kwargs):
    return pl.BlockSpec((None,) + stacked.shape[1:], lambda i: (layer, 0, 0), **kwargs)


def _inproj(x2d, g, w_all, layer, wf, bf):
    t = x2d.shape[0]
    const = lambda shape: pl.BlockSpec(shape, lambda i: (0, 0))
    return pl.pallas_call(
        _inproj_kernel,
        grid=(t // TOKEN_TILE,),
        in_specs=[
            pl.BlockSpec((TOKEN_TILE, D_MODEL), lambda i: (i, 0)),
            const((1, D_MODEL)),
            _layer_block(w_all, layer, pipeline_mode=pl.Buffered(1)),
            const((D_MODEL, LANES)),
            const((1, LANES)),
        ],
        out_specs=[
            pl.BlockSpec((TOKEN_TILE, 2 * GROUP_WIDTH), lambda i: (i, 0)),
            pl.BlockSpec((4 * GROUP_WIDTH, TOKEN_TILE), lambda i: (0, i)),
            pl.BlockSpec((TOKEN_TILE, LANES), lambda i: (i, 0)),
        ],
        out_shape=[
            jax.ShapeDtypeStruct((t, 2 * GROUP_WIDTH), jnp.bfloat16),
            jax.ShapeDtypeStruct((4 * GROUP_WIDTH, t), jnp.bfloat16),
            jax.ShapeDtypeStruct((t, LANES), jnp.float32),
        ],
        compiler_params=pltpu.CompilerParams(
            dimension_semantics=("arbitrary",), vmem_limit_bytes=VMEM_LIMIT),
        name="inproj",
    )(x2d, g, w_all, wf, bf)


def _forget_lane_layout(cols):
    rep = jnp.repeat(cols, C_PARTS, axis=-1)
    half = jnp.pad(rep, [(0, 0)] * (cols.ndim - 1) + [(0, HEAD_DIM - C_PARTS * N_HEADS)])
    return jnp.concatenate([half, half], axis=-1)


def _bias_kernel(relt_ref, bkt_ref, out_ref, *, seq):
    h = pl.program_id(0)
    bkt = bkt_ref[...]
    w = jnp.zeros(bkt.shape, jnp.float32)
    for b in range(NUM_BUCKETS):
        w = jnp.where(bkt == b, relt_ref[h, b], w)
    x = jnp.broadcast_to(w, (BLOCK, bkt.shape[1]))
    y = pltpu.roll(x, 0, 1, stride=1, stride_axis=0)
    out_ref[0] = y[:, BLOCK:].T


def _t5_bucket(dist):
    max_exact = NUM_BUCKETS // 2
    is_small = dist < max_exact
    d = jnp.maximum(dist, 1).astype(jnp.float32)
    large = max_exact + (jnp.log(d / max_exact) / math.log(MAX_DISTANCE / max_exact)
                         * (NUM_BUCKETS - max_exact)).astype(jnp.int32)
    large = jnp.minimum(large, NUM_BUCKETS - 1)
    return jnp.where(is_small, dist, large)


def _bias_strips(rel_bias, seq):
    n = jnp.arange(seq + BLOCK, dtype=jnp.int32)
    bkt = _t5_bucket(jnp.clip(seq - n, 0, seq - 1))[None, :]
    return pl.pallas_call(
        functools.partial(_bias_kernel, seq=seq),
        grid=(N_HEADS,),
        in_specs=[
            pl.BlockSpec(memory_space=pltpu.SMEM),
            pl.BlockSpec((1, seq + BLOCK), lambda h: (0, 0)),
        ],
        out_specs=pl.BlockSpec((1, seq, BLOCK), lambda h: (h, 0, 0)),
        out_shape=jax.ShapeDtypeStruct((N_HEADS, seq, BLOCK), jnp.float32),
        compiler_params=pltpu.CompilerParams(dimension_semantics=("arbitrary",)),
        name="moba_bias",
    )(rel_bias.T, bkt)


def _iota(shape, axis):
    return lax.broadcasted_iota(jnp.int32, shape, axis)


def _split3(x):
    p1 = x.astype(jnp.bfloat16)
    r1 = x - p1.astype(jnp.float32)
    p2 = r1.astype(jnp.bfloat16)
    p3 = (r1 - p2.astype(jnp.float32)).astype(jnp.bfloat16)
    return p1, p2, p3


def _augment_kv(k_ref, vt_ref, k_extra, kp_sc, vt_sc):
    seq = k_ref.shape[1]
    lane = _iota((seq, LANES), 1)
    for h in range(HEADS_IN_STEP):
        pr, hh = divmod(h, HEADS_PER_PAIR)
        hmask = (lane >= hh * HEAD_DIM) & (lane < (hh + 1) * HEAD_DIM)
        kp_sc[h] = jnp.where(hmask, k_ref[0, :, pr * LANES:(pr + 1) * LANES], k_extra)
        vt_sc[h, :HEAD_DIM, :] = vt_ref[h * HEAD_DIM:(h + 1) * HEAD_DIM, :]
        vt_sc[h, HEAD_DIM:, :] = jnp.ones((V_ROWS - HEAD_DIM, seq), vt_sc.dtype)


class _Unit:
    def __init__(self, h, i, qat, tile_term):
        self.h, self.i, self.qat, self.tile_term = h, i, qat, tile_term
        self.s_ref = self.mxp = self.m = self.acc = None


def _score_tile(u, j, kp_sc):
    key_in_tile = _iota((SLAB, BLOCK), 0)
    query_in_tile = _iota((SLAB, BLOCK), 1)
    s = jnp.dot(kp_sc[u.h, j * BLOCK:(j + 1) * BLOCK, :], u.qat, preferred_element_type=jnp.float32)
    for r in range(0, BLOCK, SLAB):
        piece = s[r:r + SLAB, :]
        if u.tile_term is not None:
            piece = piece + u.tile_term(j, r)
        if j == u.i:
            piece = jnp.where(key_in_tile + r <= query_in_tile, piece, NEG)
        u.s_ref[j * BLOCK + r:j * BLOCK + r + SLAB, :] = piece
        t = jnp.max(piece.reshape(SLAB // SUBLANES, SUBLANES, BLOCK), axis=0)
        u.mxp = t if u.mxp is None else jnp.maximum(u.mxp, t)


def _value_tile(u, j, vt_sc):
    if u.m is None:
        u.m = jnp.max(u.mxp, axis=0, keepdims=True)
    e = jnp.concatenate(
        [jnp.exp(u.s_ref[j * BLOCK + r:j * BLOCK + r + SLAB, :] - u.m).astype(jnp.bfloat16)
         for r in range(0, BLOCK, SLAB)], axis=0)
    pv = jnp.dot(vt_sc[u.h, :, j * BLOCK:(j + 1) * BLOCK], e, preferred_element_type=jnp.float32)
    u.acc = pv if u.acc is None else u.acc + pv


def _run_units(nb, make_unit, s_sc, kp_sc, vt_sc, o_ref):
    groups = [[(pr * HEADS_PER_PAIR + hh, i) for i in (a, nb - 1 - a) for hh in range(HEADS_PER_PAIR)]
              for pr in range(PAIRS_PER_STEP) for a in range(nb // 2)]
    prev = []
    for g, grp in enumerate(groups + [None]):
        cur = [make_unit(h, i) for h, i in grp] if grp is not None else []
        for n, u in enumerate(cur):
            u.s_ref = s_sc.at[(g % 2) * UNITS_PER_GROUP + n]
        for t in range(nb):
            for u in cur:
                if t <= u.i:
                    _score_tile(u, t, kp_sc)
            for u in prev:
                if t <= u.i:
                    _value_tile(u, t, vt_sc)
        for u in prev:
            out = u.acc[:HEAD_DIM, :] * (1.0 / u.acc[HEAD_DIM:HEAD_DIM + 1, :])
            o_ref[u.h * HEAD_DIM:(u.h + 1) * HEAD_DIM,
                  u.i * BLOCK:(u.i + 1) * BLOCK] = out.astype(o_ref.dtype)
        prev = cur


def _scaled_qt(qt_ref, h, i):
    pr = h // HEADS_PER_PAIR
    q = qt_ref[pr * LANES:(pr + 1) * LANES, i * BLOCK:(i + 1) * BLOCK]
    return q * jnp.asarray(SCALE, q.dtype)


def _with_spare_rows(qt, hh, spare):
    head = qt[hh * HEAD_DIM:(hh + 1) * HEAD_DIM, :]
    spare = jnp.concatenate(
        [spare, jnp.zeros((HEAD_DIM - spare.shape[0], BLOCK), qt.dtype)], axis=0)
    return jnp.concatenate([head, spare] if hh == 0 else [spare, head], axis=0)


def _attn_scratch(seq):
    return [pltpu.VMEM((2 * UNITS_PER_GROUP, seq, BLOCK), jnp.float32),
            pltpu.VMEM((HEADS_IN_STEP, seq, LANES), jnp.bfloat16),
            pltpu.VMEM((HEADS_IN_STEP, V_ROWS, seq), jnp.bfloat16)]


def _moba_kernel(qt_ref, k_ref, vt_ref, bias_ref, o_ref, s_sc, kp_sc, vt_sc, *, seq):
    nb = seq // BLOCK
    key_blk = _iota((seq, LANES), 0) // BLOCK
    onehot = ((_iota((seq, LANES), 1) % HEAD_DIM) == key_blk).astype(k_ref.dtype)
    _augment_kv(k_ref, vt_ref, onehot, kp_sc, vt_sc)
    blk_row = _iota((BF16_ROWS, BLOCK), 0)
    feat_row = _iota((LANES, BLOCK), 0)

    def pair_kmean(pr):
        k = k_ref[0, :, pr * LANES:(pr + 1) * LANES].astype(jnp.float32)
        kmean = jnp.sum(k.reshape(nb, BLOCK, LANES), axis=1) * (1.0 / BLOCK)
        return jnp.concatenate([kmean, jnp.zeros((BF16_ROWS - nb, LANES), jnp.float32)], axis=0)

    kmeans = [pair_kmean(pr) for pr in range(PAIRS_PER_STEP)]

    def make_unit(h, i):
        pr, hh = divmod(h, HEADS_PER_PAIR)
        qt = _scaled_qt(qt_ref, h, i)
        pen = jnp.zeros((BF16_ROWS, BLOCK), qt.dtype)
        if i > TOPK:
            in_head = (feat_row >= hh * HEAD_DIM) & (feat_row < (hh + 1) * HEAD_DIM)
            q_head = jnp.where(in_head, qt, jnp.zeros_like(qt)).astype(jnp.float32)
            gate = jnp.dot(kmeans[pr], q_head, precision=lax.Precision.HIGHEST,
                           preferred_element_type=jnp.float32)
            rank = jnp.zeros((BF16_ROWS, BLOCK), jnp.int32)
            for jp in range(i):
                other = gate[jp:jp + 1, :]
                beats = (other > gate) | ((other == gate) & (blk_row > jp))
                rank = rank + beats.astype(jnp.int32)
            pen = jnp.where((blk_row < i) & (rank >= TOPK), NEG, 0.0).astype(qt.dtype)

        def tile_term(j, r):
            off = (nb - 1 - i + j) * BLOCK + r
            return bias_ref[h, off:off + SLAB, :]

        return _Unit(h, i, _with_spare_rows(qt, hh, pen), tile_term)

    _run_units(nb, make_unit, s_sc, kp_sc, vt_sc, o_ref)


def _moba(k3, qvt, bias):
    b, seq, _ = k3.shape
    return pl.pallas_call(
        functools.partial(_moba_kernel, seq=seq),
        grid=(N_STEPS, b),
        in_specs=[
            pl.BlockSpec((STEP_WIDTH, seq), lambda p, bi: (p, bi)),
            pl.BlockSpec((1, seq, STEP_WIDTH), lambda p, bi: (bi, 0, p)),
            pl.BlockSpec((STEP_WIDTH, seq), lambda p, bi: (2 * N_STEPS + p, bi)),
            pl.BlockSpec((HEADS_IN_STEP, seq, BLOCK), lambda p, bi: (p, 0, 0),
                         pipeline_mode=pl.Buffered(1)),
        ],
        out_specs=pl.BlockSpec((STEP_WIDTH, seq), lambda p, bi: (p, bi)),
        out_shape=jax.ShapeDtypeStruct((GROUP_WIDTH, b * seq), jnp.bfloat16),
        scratch_shapes=_attn_scratch(seq),
        compiler_params=pltpu.CompilerParams(
            dimension_semantics=("arbitrary", "arbitrary"), vmem_limit_bytes=VMEM_LIMIT),
        name="moba_attn",
    )(qvt, k3, qvt, bias)


def _fox_kernel(qt_ref, k_ref, vt_ref, logf_ref, o_ref, ext_sc, s_sc, kp_sc, vt_sc, *, seq):
    nb = seq // BLOCK
    p = pl.program_id(1)

    @pl.when(p == 0)
    def _():
        tri = (_iota((BLOCK, BLOCK), 1) <= _iota((BLOCK, BLOCK), 0)).astype(jnp.bfloat16)
        lane = _iota((BLOCK, LANES), 1) % HEAD_DIM
        carry = jnp.zeros((1, LANES), jnp.float32)
        for ch in range(nb):
            parts = _split3(logf_ref[ch * BLOCK:(ch + 1) * BLOCK, :])
            cs = jnp.dot(tri, jnp.concatenate(parts, axis=1), preferred_element_type=jnp.float32)
            c = cs[:, :LANES] + cs[:, LANES:2 * LANES] + cs[:, 2 * LANES:] + carry
            carry = c[BLOCK - 1:BLOCK, :]
            p1, p2, p3 = _split3(-c)
            part = lane % C_PARTS
            ext = jnp.where(part == 0, p1, jnp.where(part == 1, p2, p3))
            ext_sc[ch * BLOCK:(ch + 1) * BLOCK, :] = jnp.where(
                lane < C_PARTS * N_HEADS, ext, jnp.zeros_like(ext))

    _augment_kv(k_ref, vt_ref, ext_sc[...], kp_sc, vt_sc)
    n_spare = -(-C_PARTS * N_HEADS // BF16_ROWS) * BF16_ROWS
    row = _iota((n_spare, BLOCK), 0)

    def make_unit(h, i):
        qt = _scaled_qt(qt_ref, h, i)
        first = C_PARTS * (HEADS_IN_STEP * p + h)
        spare = jnp.where((row >= first) & (row < first + C_PARTS), 1.0, 0.0).astype(qt.dtype)
        return _Unit(h, i, _with_spare_rows(qt, h % HEADS_PER_PAIR, spare), None)

    _run_units(nb, make_unit, s_sc, kp_sc, vt_sc, o_ref)


def _fox(k3, qvt, logf3):
    b, seq, _ = k3.shape
    return pl.pallas_call(
        functools.partial(_fox_kernel, seq=seq),
        grid=(b, N_STEPS),
        in_specs=[
            pl.BlockSpec((STEP_WIDTH, seq), lambda bi, p: (N_STEPS + p, bi)),
            pl.BlockSpec((1, seq, STEP_WIDTH), lambda bi, p: (bi, 0, N_STEPS + p)),
            pl.BlockSpec((STEP_WIDTH, seq), lambda bi, p: (3 * N_STEPS + p, bi)),
            pl.BlockSpec((None, seq, LANES), lambda bi, p: (bi, 0, 0)),
        ],
        out_specs=pl.BlockSpec((STEP_WIDTH, seq), lambda bi, p: (p, bi)),
        out_shape=jax.ShapeDtypeStruct((GROUP_WIDTH, b * seq), jnp.bfloat16),
        scratch_shapes=[pltpu.VMEM((seq, LANES), jnp.bfloat16)] + _attn_scratch(seq),
        compiler_params=pltpu.CompilerParams(
            dimension_semantics=("arbitrary", "arbitrary"), vmem_limit_bytes=VMEM_LIMIT),
        name="fox_attn",
    )(qvt, k3, qvt, logf3)


def _dense_kernel(x_ref, ymt_ref, yft_ref, wo_ref, g_ref, wgu_ref, wd_ref, gfin_ref, o_ref, *, final):
    x1 = (x_ref[...]
          + lax.dot_general(ymt_ref[...], wo_ref[:GROUP_WIDTH, :].astype(jnp.bfloat16), _TN,
                            preferred_element_type=jnp.float32)
          + lax.dot_general(yft_ref[...], wo_ref[GROUP_WIDTH:, :].astype(jnp.bfloat16), _TN,
                            preferred_element_type=jnp.float32))
    h = _rms(x1, g_ref[...]).astype(jnp.bfloat16)
    acc = jnp.zeros_like(x1)
    for c in range(D_FF // FF_CHUNK):
        lo = c * FF_CHUNK
        gate = jnp.dot(h, wgu_ref[:, lo:lo + FF_CHUNK], preferred_element_type=jnp.float32)
        up = jnp.dot(h, wgu_ref[:, D_FF + lo:D_FF + lo + FF_CHUNK], preferred_element_type=jnp.float32)
        a = (gate / (1.0 + jnp.exp(-gate)) * up).astype(jnp.bfloat16)
        acc = acc + jnp.dot(a, wd_ref[lo:lo + FF_CHUNK, :], preferred_element_type=jnp.float32)
    acc = acc + x1
    if final:
        acc = _rms(acc, gfin_ref[...])
    o_ref[...] = acc


def _dense(x2d, ymt, yft, wo_all, g, wgu_all, wd_all, gfin, layer, final):
    t = x2d.shape[0]
    const = lambda shape: pl.BlockSpec(shape, lambda i: (0,) * len(shape),
                                       pipeline_mode=pl.Buffered(1))
    weights = lambda stacked: _layer_block(stacked, layer, pipeline_mode=pl.Buffered(1))
    return pl.pallas_call(
        functools.partial(_dense_kernel, final=final),
        grid=(t // TOKEN_TILE,),
        in_specs=[
            pl.BlockSpec((TOKEN_TILE, D_MODEL), lambda i: (i, 0)),
            pl.BlockSpec((GROUP_WIDTH, TOKEN_TILE), lambda i: (0, i)),
            pl.BlockSpec((GROUP_WIDTH, TOKEN_TILE), lambda i: (0, i)),
            weights(wo_all),
            const((1, D_MODEL)),
            weights(wgu_all),
            weights(wd_all),
            const((1, D_MODEL)),
        ],
        out_specs=pl.BlockSpec((TOKEN_TILE, D_MODEL), lambda i: (i, 0)),
        out_shape=jax.ShapeDtypeStruct((t, D_MODEL), jnp.float32),
        compiler_params=pltpu.CompilerParams(
            dimension_semantics=("arbitrary",), vmem_limit_bytes=VMEM_LIMIT),
        name="dense_ffn",
    )(x2d, ymt, yft, wo_all, g, wgu_all, wd_all, gfin)


def kernel(x, w_in, b_f, w_o, g_attn, w_gu, w_down, g_ffn, rel_bias, g_final):
    b, seq, d = x.shape
    depth = w_in.shape[0]
    assert d == D_MODEL and seq % BLOCK == 0 and (b * seq) % TOKEN_TILE == 0
    assert seq // BLOCK <= BF16_ROWS and C_PARTS * N_HEADS <= HEAD_DIM
    bf16 = jnp.bfloat16
    gw = GROUP_WIDTH
    bias = _bias_strips(rel_bias, seq)
    w_gu16, w_down16 = w_gu.astype(bf16), w_down.astype(bf16)
    w_f = _forget_lane_layout(w_in[:, :, 6 * gw:]).astype(bf16)
    b_lanes = _forget_lane_layout(b_f[:, None, :])
    x2d = x.reshape(b * seq, d)
    for layer in range(depth):
        k2, qvt, logf = _inproj(x2d, g_attn[layer][None, :], w_in, layer, w_f[layer], b_lanes[layer])
        k3 = k2.reshape(b, seq, 2 * gw)
        ymt = _moba(k3, qvt, bias)
        yft = _fox(k3, qvt, logf.reshape(b, seq, LANES))
        x2d = _dense(x2d, ymt, yft, w_o, g_ffn[layer][None, :], w_gu16, w_down16,
                     g_final[None, :], layer, final=(layer == depth - 1))
    return x2d.reshape(b, seq, d)
```

```python
import functools
import math

import jax
import jax.numpy as jnp
from jax import lax
from jax.experimental import pallas as pl
from jax.experimental.pallas import tpu as pltpu

D_MODEL = 1024
HEAD_DIM = 64
N_HEADS = 8
GROUP_WIDTH = N_HEADS * HEAD_DIM
BLOCK = 256
TOPK = 3
NUM_BUCKETS = 32
MAX_DISTANCE = 1024
D_FF = 2816
RMS_EPS = 1e-6
SCALE = HEAD_DIM ** -0.5
NEG = -1e30

LANES = 128
SUBLANES = 8
BF16_ROWS = 16
HEADS_PER_PAIR = LANES // HEAD_DIM
PAIRS_PER_STEP = 2
HEADS_IN_STEP = PAIRS_PER_STEP * HEADS_PER_PAIR
STEP_WIDTH = PAIRS_PER_STEP * LANES
N_STEPS = GROUP_WIDTH // STEP_WIDTH
UNITS_PER_GROUP = 2 * HEADS_PER_PAIR
C_PARTS = 3
V_ROWS = HEAD_DIM + BF16_ROWS
SLAB = 16
FF_CHUNK = 256
TOKEN_TILE = 1024
VMEM_LIMIT = 56 * 1024 * 1024

_NT = (((1,), (1,)), ((), ()))
_TN = (((0,), (0,)), ((), ()))


def _rms(x, g):
    return x * lax.rsqrt(jnp.mean(x * x, axis=-1, keepdims=True) + RMS_EPS) * g


def _inproj_kernel(x_ref, g_ref, w_ref, wf_ref, bf_ref, k_ref, qvt_ref, logf_ref):
    gw = GROUP_WIDTH
    h = _rms(x_ref[...], g_ref[...]).astype(jnp.bfloat16)

    def proj(group):
        return jnp.dot(h, w_ref[:, group * gw:(group + 1) * gw], preferred_element_type=jnp.float32)

    for n, group in enumerate((1, 4)):
        k_ref[:, n * gw:(n + 1) * gw] = proj(group).astype(jnp.bfloat16)
    for n, group in enumerate((0, 3, 2, 5)):
        qvt_ref[n * gw:(n + 1) * gw, :] = proj(group).T.astype(jnp.bfloat16)
    z = jnp.dot(h, wf_ref[...], preferred_element_type=jnp.float32) + bf_ref[...]
    logf_ref[...] = -(jnp.maximum(-z, 0.0) + jnp.log1p(jnp.exp(-jnp.abs(z))))


def _layer_block(stacked, layer, **kwargs):
    return pl.BlockSpec((None,) + stacked.shape[1:], lambda i: (layer, 0, 0), **kwargs)


def _inproj(x2d, g, w_all, layer, wf, bf):
    t = x2d.shape[0]
    const = lambda shape: pl.BlockSpec(shape, lambda i: (0, 0))
    return pl.pallas_call(
        _inproj_kernel,
        grid=(t // TOKEN_TILE,),
        in_specs=[
            pl.BlockSpec((TOKEN_TILE, D_MODEL), lambda i: (i, 0)),
            const((1, D_MODEL)),
            _layer_block(w_all, layer),
            const((D_MODEL, LANES)),
            const((1, LANES)),
        ],
        out_specs=[
            pl.BlockSpec((TOKEN_TILE, 2 * GROUP_WIDTH), lambda i: (i, 0)),
            pl.BlockSpec((4 * GROUP_WIDTH, TOKEN_TILE), lambda i: (0, i)),
            pl.BlockSpec((TOKEN_TILE, LANES), lambda i: (i, 0)),
        ],
        out_shape=[
            jax.ShapeDtypeStruct((t, 2 * GROUP_WIDTH), jnp.bfloat16),
            jax.ShapeDtypeStruct((4 * GROUP_WIDTH, t), jnp.bfloat16),
            jax.ShapeDtypeStruct((t, LANES), jnp.float32),
        ],
        compiler_params=pltpu.CompilerParams(
            dimension_semantics=("arbitrary",), vmem_limit_bytes=VMEM_LIMIT),
        name="inproj",
    )(x2d, g, w_all, wf, bf)


def _forget_lane_layout(cols):
    rep = jnp.repeat(cols, C_PARTS, axis=-1)
    half = jnp.pad(rep, [(0, 0)] * (cols.ndim - 1) + [(0, HEAD_DIM - C_PARTS * N_HEADS)])
    return jnp.concatenate([half, half], axis=-1)


def _bias_kernel(relt_ref, bkt_ref, out_ref, *, seq):
    h = pl.program_id(0)
    bkt = bkt_ref[...]
    w = jnp.zeros(bkt.shape, jnp.float32)
    for b in range(NUM_BUCKETS):
        w = jnp.where(bkt == b, relt_ref[h, b], w)
    x = jnp.broadcast_to(w, (BLOCK, bkt.shape[1]))
    y = pltpu.roll(x, 0, 1, stride=1, stride_axis=0)
    out_ref[0] = y[:, BLOCK:].T


def _t5_bucket(dist):
    max_exact = NUM_BUCKETS // 2
    is_small = dist < max_exact
    d = jnp.maximum(dist, 1).astype(jnp.float32)
    large = max_exact + (jnp.log(d / max_exact) / math.log(MAX_DISTANCE / max_exact)
                         * (NUM_BUCKETS - max_exact)).astype(jnp.int32)
    large = jnp.minimum(large, NUM_BUCKETS - 1)
    return jnp.where(is_small, dist, large)


def _bias_strips(rel_bias, seq):
    n = jnp.arange(seq + BLOCK, dtype=jnp.int32)
    bkt = _t5_bucket(jnp.clip(seq - n, 0, seq - 1))[None, :]
    return pl.pallas_call(
        functools.partial(_bias_kernel, seq=seq),
        grid=(N_HEADS,),
        in_specs=[
            pl.BlockSpec(memory_space=pltpu.SMEM),
            pl.BlockSpec((1, seq + BLOCK), lambda h: (0, 0)),
        ],
        out_specs=pl.BlockSpec((1, seq, BLOCK), lambda h: (h, 0, 0)),
        out_shape=jax.ShapeDtypeStruct((N_HEADS, seq, BLOCK), jnp.float32),
        compiler_params=pltpu.CompilerParams(dimension_semantics=("arbitrary",)),
        name="moba_bias",
    )(rel_bias.T, bkt)


def _iota(shape, axis):
    return lax.broadcasted_iota(jnp.int32, shape, axis)


def _split3(x):
    p1 = x.astype(jnp.bfloat16)
    r1 = x - p1.astype(jnp.float32)
    p2 = r1.astype(jnp.bfloat16)
    p3 = (r1 - p2.astype(jnp.float32)).astype(jnp.bfloat16)
    return p1, p2, p3


def _augment_kv(k_ref, vt_ref, k_extra, kp_sc, vt_sc):
    seq = k_ref.shape[1]
    lane = _iota((seq, LANES), 1)
    for h in range(HEADS_IN_STEP):
        pr, hh = divmod(h, HEADS_PER_PAIR)
        hmask = (lane >= hh * HEAD_DIM) & (lane < (hh + 1) * HEAD_DIM)
        kp_sc[h] = jnp.where(hmask, k_ref[0, :, pr * LANES:(pr + 1) * LANES], k_extra)
        vt_sc[h, :HEAD_DIM, :] = vt_ref[h * HEAD_DIM:(h + 1) * HEAD_DIM, :]
        vt_sc[h, HEAD_DIM:, :] = jnp.ones((V_ROWS - HEAD_DIM, seq), vt_sc.dtype)


class _Unit:
    def __init__(self, h, i, qat, tile_term):
        self.h, self.i, self.qat, self.tile_term = h, i, qat, tile_term
        self.s_ref = self.mxp = self.m = self.acc = None


def _score_tile(u, j, kp_sc):
    key_in_tile = _iota((SLAB, BLOCK), 0)
    query_in_tile = _iota((SLAB, BLOCK), 1)
    s = jnp.dot(kp_sc[u.h, j * BLOCK:(j + 1) * BLOCK, :], u.qat, preferred_element_type=jnp.float32)
    for r in range(0, BLOCK, SLAB):
        piece = s[r:r + SLAB, :]
        if u.tile_term is not None:
            piece = piece + u.tile_term(j, r)
        if j == u.i:
            piece = jnp.where(key_in_tile + r <= query_in_tile, piece, NEG)
        u.s_ref[j * BLOCK + r:j * BLOCK + r + SLAB, :] = piece
        t = jnp.max(piece.reshape(SLAB // SUBLANES, SUBLANES, BLOCK), axis=0)
        u.mxp = t if u.mxp is None else jnp.maximum(u.mxp, t)


def _value_tile(u, j, vt_sc):
    if u.m is None:
        u.m = jnp.max(u.mxp, axis=0, keepdims=True)
    e = jnp.concatenate(
        [jnp.exp(u.s_ref[j * BLOCK + r:j * BLOCK + r + SLAB, :] - u.m).astype(jnp.bfloat16)
         for r in range(0, BLOCK, SLAB)], axis=0)
    pv = jnp.dot(vt_sc[u.h, :, j * BLOCK:(j + 1) * BLOCK], e, preferred_element_type=jnp.float32)
    u.acc = pv if u.acc is None else u.acc + pv


def _run_units(nb, make_unit, s_sc, kp_sc, vt_sc, o_ref):
    groups = [[(pr * HEADS_PER_PAIR + hh, i) for i in (a, nb - 1 - a) for hh in range(HEADS_PER_PAIR)]
              for pr in range(PAIRS_PER_STEP) for a in range(nb // 2)]
    prev = []
    for g, grp in enumerate(groups + [None]):
        cur = [make_unit(h, i) for h, i in grp] if grp is not None else []
        for n, u in enumerate(cur):
            u.s_ref = s_sc.at[(g % 2) * UNITS_PER_GROUP + n]
        for t in range(nb):
            for u in cur:
                if t <= u.i:
                    _score_tile(u, t, kp_sc)
            for u in prev:
                if t <= u.i:
                    _value_tile(u, t, vt_sc)
        for u in prev:
            out = u.acc[:HEAD_DIM, :] * (1.0 / u.acc[HEAD_DIM:HEAD_DIM + 1, :])
            o_ref[u.h * HEAD_DIM:(u.h + 1) * HEAD_DIM,
                  u.i * BLOCK:(u.i + 1) * BLOCK] = out.astype(o_ref.dtype)
        prev = cur


def _scaled_qt(qt_ref, h, i):
    pr = h // HEADS_PER_PAIR
    q = qt_ref[pr * LANES:(pr + 1) * LANES, i * BLOCK:(i + 1) * BLOCK]
    return q * jnp.asarray(SCALE, q.dtype)


def _with_spare_rows(qt, hh, spare):
    head = qt[hh * HEAD_DIM:(hh + 1) * HEAD_DIM, :]
    spare = jnp.concatenate(
        [spare, jnp.zeros((HEAD_DIM - spare.shape[0], BLOCK), qt.dtype)], axis=0)
    return jnp.concatenate([head, spare] if hh == 0 else [spare, head], axis=0)


def _attn_scratch(seq):
    return [pltpu.VMEM((2 * UNITS_PER_GROUP, seq, BLOCK), jnp.float32),
            pltpu.VMEM((HEADS_IN_STEP, seq, LANES), jnp.bfloat16),
            pltpu.VMEM((HEADS_IN_STEP, V_ROWS, seq), jnp.bfloat16)]


def _moba_kernel(qt_ref, k_ref, vt_ref, bias_ref, o_ref, s_sc, kp_sc, vt_sc, *, seq):
    nb = seq // BLOCK
    key_blk = _iota((seq, LANES), 0) // BLOCK
    onehot = ((_iota((seq, LANES), 1) % HEAD_DIM) == key_blk).astype(k_ref.dtype)
    _augment_kv(k_ref, vt_ref, onehot, kp_sc, vt_sc)
    blk_row = _iota((BF16_ROWS, BLOCK), 0)
    feat_row = _iota((LANES, BLOCK), 0)

    def pair_kmean(pr):
        k = k_ref[0, :, pr * LANES:(pr + 1) * LANES].astype(jnp.float32)
        kmean = jnp.sum(k.reshape(nb, BLOCK, LANES), axis=1) * (1.0 / BLOCK)
        return jnp.concatenate([kmean, jnp.zeros((BF16_ROWS - nb, LANES), jnp.float32)], axis=0)

    kmeans = [pair_kmean(pr) for pr in range(PAIRS_PER_STEP)]

    def make_unit(h, i):
        pr, hh = divmod(h, HEADS_PER_PAIR)
        qt = _scaled_qt(qt_ref, h, i)
        pen = jnp.zeros((BF16_ROWS, BLOCK), qt.dtype)
        if i > TOPK:
            in_head = (feat_row >= hh * HEAD_DIM) & (feat_row < (hh + 1) * HEAD_DIM)
            q_head = jnp.where(in_head, qt, jnp.zeros_like(qt)).astype(jnp.float32)
            gate = jnp.dot(kmeans[pr], q_head, precision=lax.Precision.HIGHEST,
                           preferred_element_type=jnp.float32)
            rank = jnp.zeros((BF16_ROWS, BLOCK), jnp.int32)
            for jp in range(i):
                other = gate[jp:jp + 1, :]
                beats = (other > gate) | ((other == gate) & (blk_row > jp))
                rank = rank + beats.astype(jnp.int32)
            pen = jnp.where((blk_row < i) & (rank >= TOPK), NEG, 0.0).astype(qt.dtype)

        def tile_term(j, r):
            off = (nb - 1 - i + j) * BLOCK + r
            return bias_ref[h, off:off + SLAB, :]

        return _Unit(h, i, _with_spare_rows(qt, hh, pen), tile_term)

    _run_units(nb, make_unit, s_sc, kp_sc, vt_sc, o_ref)


def _moba(k3, qvt, bias):
    b, seq, _ = k3.shape
    return pl.pallas_call(
        functools.partial(_moba_kernel, seq=seq),
        grid=(N_STEPS, b),
        in_specs=[
            pl.BlockSpec((STEP_WIDTH, seq), lambda p, bi: (p, bi)),
            pl.BlockSpec((1, seq, STEP_WIDTH), lambda p, bi: (bi, 0, p)),
            pl.BlockSpec((STEP_WIDTH, seq), lambda p, bi: (2 * N_STEPS + p, bi)),
            pl.BlockSpec((HEADS_IN_STEP, seq, BLOCK), lambda p, bi: (p, 0, 0),
                         pipeline_mode=pl.Buffered(1)),
        ],
        out_specs=pl.BlockSpec((STEP_WIDTH, seq), lambda p, bi: (p, bi)),
        out_shape=jax.ShapeDtypeStruct((GROUP_WIDTH, b * seq), jnp.bfloat16),
        scratch_shapes=_attn_scratch(seq),
        compiler_params=pltpu.CompilerParams(
            dimension_semantics=("arbitrary", "arbitrary"), vmem_limit_bytes=VMEM_LIMIT),
        name="moba_attn",
    )(qvt, k3, qvt, bias)


def _fox_kernel(qt_ref, k_ref, vt_ref, logf_ref, o_ref, ext_sc, s_sc, kp_sc, vt_sc, *, seq):
    nb = seq // BLOCK
    p = pl.program_id(1)

    @pl.when(p == 0)
    def _():
        tri = (_iota((BLOCK, BLOCK), 1) <= _iota((BLOCK, BLOCK), 0)).astype(jnp.bfloat16)
        lane = _iota((BLOCK, LANES), 1) % HEAD_DIM
        carry = jnp.zeros((1, LANES), jnp.float32)
        for ch in range(nb):
            parts = _split3(logf_ref[ch * BLOCK:(ch + 1) * BLOCK, :])
            cs = jnp.dot(tri, jnp.concatenate(parts, axis=1), preferred_element_type=jnp.float32)
            c = cs[:, :LANES] + cs[:, LANES:2 * LANES] + cs[:, 2 * LANES:] + carry
            carry = c[BLOCK - 1:BLOCK, :]
            p1, p2, p3 = _split3(-c)
            part = lane % C_PARTS
            ext = jnp.where(part == 0, p1, jnp.where(part == 1, p2, p3))
            ext_sc[ch * BLOCK:(ch + 1) * BLOCK, :] = jnp.where(
                lane < C_PARTS * N_HEADS, ext, jnp.zeros_like(ext))

    _augment_kv(k_ref, vt_ref, ext_sc[...], kp_sc, vt_sc)
    n_spare = -(-C_PARTS * N_HEADS // BF16_ROWS) * BF16_ROWS
    row = _iota((n_spare, BLOCK), 0)

    def make_unit(h, i):
        qt = _scaled_qt(qt_ref, h, i)
        first = C_PARTS * (HEADS_IN_STEP * p + h)
        spare = jnp.where((row >= first) & (row < first + C_PARTS), 1.0, 0.0).astype(qt.dtype)
        return _Unit(h, i, _with_spare_rows(qt, h % HEADS_PER_PAIR, spare), None)

    _run_units(nb, make_unit, s_sc, kp_sc, vt_sc, o_ref)


def _fox(k3, qvt, logf3):
    b, seq, _ = k3.shape
    return pl.pallas_call(
        functools.partial(_fox_kernel, seq=seq),
        grid=(b, N_STEPS),
        in_specs=[
            pl.BlockSpec((STEP_WIDTH, seq), lambda bi, p: (N_STEPS + p, bi)),
            pl.BlockSpec((1, seq, STEP_WIDTH), lambda bi, p: (bi, 0, N_STEPS + p)),
            pl.BlockSpec((STEP_WIDTH, seq), lambda bi, p: (3 * N_STEPS + p, bi)),
            pl.BlockSpec((None, seq, LANES), lambda bi, p: (bi, 0, 0)),
        ],
        out_specs=pl.BlockSpec((STEP_WIDTH, seq), lambda bi, p: (p, bi)),
        out_shape=jax.ShapeDtypeStruct((GROUP_WIDTH, b * seq), jnp.bfloat16),
        scratch_shapes=[pltpu.VMEM((seq, LANES), jnp.bfloat16)] + _attn_scratch(seq),
        compiler_params=pltpu.CompilerParams(
            dimension_semantics=("arbitrary", "arbitrary"), vmem_limit_bytes=VMEM_LIMIT),
        name="fox_attn",
    )(qvt, k3, qvt, logf3)


def _dense_kernel(x_ref, ymt_ref, yft_ref, wo_ref, g_ref, wgu_ref, wd_ref, gfin_ref, o_ref, *, final):
    x1 = (x_ref[...]
          + lax.dot_general(ymt_ref[...], wo_ref[:GROUP_WIDTH, :], _TN,
                            preferred_element_type=jnp.float32)
          + lax.dot_general(yft_ref[...], wo_ref[GROUP_WIDTH:, :], _TN,
                            preferred_element_type=jnp.float32))
    h = _rms(x1, g_ref[...]).astype(jnp.bfloat16)
    acc = jnp.zeros_like(x1)
    for c in range(D_FF // FF_CHUNK):
        lo = c * FF_CHUNK
        gate = jnp.dot(h, wgu_ref[:, lo:lo + FF_CHUNK], preferred_element_type=jnp.float32)
        up = jnp.dot(h, wgu_ref[:, D_FF + lo:D_FF + lo + FF_CHUNK], preferred_element_type=jnp.float32)
        a = (gate / (1.0 + jnp.exp(-gate)) * up).astype(jnp.bfloat16)
        acc = acc + jnp.dot(a, wd_ref[lo:lo + FF_CHUNK, :], preferred_element_type=jnp.float32)
    acc = acc + x1
    if final:
        acc = _rms(acc, gfin_ref[...])
    o_ref[...] = acc


def _dense(x2d, ymt, yft, wo_all, g, wgu_all, wd_all, gfin, layer, final):
    t = x2d.shape[0]
    const = lambda shape: pl.BlockSpec(shape, lambda i: (0,) * len(shape),
                                       pipeline_mode=pl.Buffered(1))
    weights = lambda stacked: _layer_block(stacked, layer, pipeline_mode=pl.Buffered(1))
    return pl.pallas_call(
        functools.partial(_dense_kernel, final=final),
        grid=(t // TOKEN_TILE,),
        in_specs=[
            pl.BlockSpec((TOKEN_TILE, D_MODEL), lambda i: (i, 0)),
            pl.BlockSpec((GROUP_WIDTH, TOKEN_TILE), lambda i: (0, i)),
            pl.BlockSpec((GROUP_WIDTH, TOKEN_TILE), lambda i: (0, i)),
            weights(wo_all),
            const((1, D_MODEL)),
            weights(wgu_all),
            weights(wd_all),
            const((1, D_MODEL)),
        ],
        out_specs=pl.BlockSpec((TOKEN_TILE, D_MODEL), lambda i: (i, 0)),
        out_shape=jax.ShapeDtypeStruct((t, D_MODEL), jnp.float32),
        compiler_params=pltpu.CompilerParams(
            dimension_semantics=("arbitrary",), vmem_limit_bytes=VMEM_LIMIT),
        name="dense_ffn",
    )(x2d, ymt, yft, wo_all, g, wgu_all, wd_all, gfin)


def kernel(x, w_in, b_f, w_o, g_attn, w_gu, w_down, g_ffn, rel_bias, g_final):
    b, seq, d = x.shape
    depth = w_in.shape[0]
    assert d == D_MODEL and seq % BLOCK == 0 and (b * seq) % TOKEN_TILE == 0
    assert seq // BLOCK <= BF16_ROWS and C_PARTS * N_HEADS <= HEAD_DIM
    bf16 = jnp.bfloat16
    gw = GROUP_WIDTH
    bias = _bias_strips(rel_bias, seq)
    w_o16, w_gu16, w_down16 = (w.astype(bf16) for w in (w_o, w_gu, w_down))
    w_in16 = w_in[:, :, :6 * gw].astype(bf16)
    w_f = _forget_lane_layout(w_in[:, :, 6 * gw:]).astype(bf16)
    b_lanes = _forget_lane_layout(b_f[:, None, :])
    x2d = x.reshape(b * seq, d)
    for layer in range(depth):
        k2, qvt, logf = _inproj(x2d, g_attn[layer][None, :], w_in16, layer, w_f[layer], b_lanes[layer])
        k3 = k2.reshape(b, seq, 2 * gw)
        ymt = _moba(k3, qvt, bias)
        yft = _fox(k3, qvt, logf.reshape(b, seq, LANES))
        x2d = _dense(x2d, ymt, yft, w_o16, g_ffn[layer][None, :], w_gu16, w_down16,
                     g_final[None, :], layer, final=(layer == depth - 1))
    return x2d.reshape(b, seq, d)
```

```python
import functools
import math

import jax
import jax.numpy as jnp
from jax import lax
from jax.experimental import pallas as pl
from jax.experimental.pallas import tpu as pltpu

D_MODEL = 1024
HEAD_DIM = 64
N_HEADS = 8
GROUP_WIDTH = N_HEADS * HEAD_DIM
BLOCK = 256
TOPK = 3
NUM_BUCKETS = 32
MAX_DISTANCE = 1024
D_FF = 2816
RMS_EPS = 1e-6
SCALE = HEAD_DIM ** -0.5
NEG = -1e30

LANES = 128
SUBLANES = 8
BF16_ROWS = 16
HEADS_PER_PAIR = LANES // HEAD_DIM
PAIRS_PER_STEP = 2
HEADS_IN_STEP = PAIRS_PER_STEP * HEADS_PER_PAIR
STEP_WIDTH = PAIRS_PER_STEP * LANES
N_STEPS = GROUP_WIDTH // STEP_WIDTH
UNITS_PER_GROUP = 2 * HEADS_PER_PAIR
C_PARTS = 3
V_ROWS = HEAD_DIM + BF16_ROWS
SLAB = 16
FF_CHUNK = 256
TOKEN_TILE = 1024
VMEM_LIMIT = 56 * 1024 * 1024

_NT = (((1,), (1,)), ((), ()))
_TN = (((0,), (0,)), ((), ()))


def _rms(x, g):
    return x * lax.rsqrt(jnp.mean(x * x, axis=-1, keepdims=True) + RMS_EPS) * g


def _inproj_kernel(x_ref, g_ref, w_ref, wf_ref, bf_ref, k_ref, qvt_ref, logf_ref):
    gw = GROUP_WIDTH
    h = _rms(x_ref[...], g_ref[...]).astype(jnp.bfloat16)

    def proj(group):
        return jnp.dot(h, w_ref[:, group * gw:(group + 1) * gw], preferred_element_type=jnp.float32)

    for n, group in enumerate((1, 4)):
        k_ref[:, n * gw:(n + 1) * gw] = proj(group).astype(jnp.bfloat16)
    for n, group in enumerate((0, 3, 2, 5)):
        qvt_ref[n * gw:(n + 1) * gw, :] = proj(group).T.astype(jnp.bfloat16)
    z = jnp.dot(h, wf_ref[...], preferred_element_type=jnp.float32) + bf_ref[...]
    logf_ref[...] = -(jnp.maximum(-z, 0.0) + jnp.log1p(jnp.exp(-jnp.abs(z))))


def _layer_block(stacked, layer, **kwargs):
    return pl.BlockSpec((None,) + stacked.shape[1:], lambda i: (layer, 0, 0), **kwargs)


def _inproj(x2d, g, w_all, layer, wf, bf):
    t = x2d.shape[0]
    const = lambda shape: pl.BlockSpec(shape, lambda i: (0, 0))
    return pl.pallas_call(
        _inproj_kernel,
        grid=(t // TOKEN_TILE,),
        in_specs=[
            pl.BlockSpec((TOKEN_TILE, D_MODEL), lambda i: (i, 0)),
            const((1, D_MODEL)),
            _layer_block(w_all, layer),
            const((D_MODEL, LANES)),
            const((1, LANES)),
        ],
        out_specs=[
            pl.BlockSpec((TOKEN_TILE, 2 * GROUP_WIDTH), lambda i: (i, 0)),
            pl.BlockSpec((4 * GROUP_WIDTH, TOKEN_TILE), lambda i: (0, i)),
            pl.BlockSpec((TOKEN_TILE, LANES), lambda i: (i, 0)),
        ],
        out_shape=[
            jax.ShapeDtypeStruct((t, 2 * GROUP_WIDTH), jnp.bfloat16),
            jax.ShapeDtypeStruct((4 * GROUP_WIDTH, t), jnp.bfloat16),
            jax.ShapeDtypeStruct((t, LANES), jnp.float32),
        ],
        compiler_params=pltpu.CompilerParams(
            dimension_semantics=("arbitrary",), vmem_limit_bytes=VMEM_LIMIT),
        name="inproj",
    )(x2d, g, w_all, wf, bf)


def _forget_lane_layout(cols):
    rep = jnp.repeat(cols, C_PARTS, axis=-1)
    half = jnp.pad(rep, [(0, 0)] * (cols.ndim - 1) + [(0, HEAD_DIM - C_PARTS * N_HEADS)])
    return jnp.concatenate([half, half], axis=-1)


def _bias_kernel(relt_ref, bkt_ref, out_ref, *, seq):
    h = pl.program_id(0)
    bkt = bkt_ref[...]
    w = jnp.zeros(bkt.shape, jnp.float32)
    for b in range(NUM_BUCKETS):
        w = jnp.where(bkt == b, relt_ref[h, b], w)
    x = jnp.broadcast_to(w, (BLOCK, bkt.shape[1]))
    y = pltpu.roll(x, 0, 1, stride=1, stride_axis=0)
    out_ref[0] = y[:, BLOCK:].T


def _t5_bucket(dist):
    max_exact = NUM_BUCKETS // 2
    is_small = dist < max_exact
    d = jnp.maximum(dist, 1).astype(jnp.float32)
    large = max_exact + (jnp.log(d / max_exact) / math.log(MAX_DISTANCE / max_exact)
                         * (NUM_BUCKETS - max_exact)).astype(jnp.int32)
    large = jnp.minimum(large, NUM_BUCKETS - 1)
    return jnp.where(is_small, dist, large)


def _bias_strips(rel_bias, seq):
    n = jnp.arange(seq + BLOCK, dtype=jnp.int32)
    bkt = _t5_bucket(jnp.clip(seq - n, 0, seq - 1))[None, :]
    return pl.pallas_call(
        functools.partial(_bias_kernel, seq=seq),
        grid=(N_HEADS,),
        in_specs=[
            pl.BlockSpec(memory_space=pltpu.SMEM),
            pl.BlockSpec((1, seq + BLOCK), lambda h: (0, 0)),
        ],
        out_specs=pl.BlockSpec((1, seq, BLOCK), lambda h: (h, 0, 0)),
        out_shape=jax.ShapeDtypeStruct((N_HEADS, seq, BLOCK), jnp.float32),
        compiler_params=pltpu.CompilerParams(dimension_semantics=("arbitrary",)),
        name="moba_bias",
    )(rel_bias.T, bkt)


def _iota(shape, axis):
    return lax.broadcasted_iota(jnp.int32, shape, axis)


def _split3(x):
    p1 = x.astype(jnp.bfloat16)
    r1 = x - p1.astype(jnp.float32)
    p2 = r1.astype(jnp.bfloat16)
    p3 = (r1 - p2.astype(jnp.float32)).astype(jnp.bfloat16)
    return p1, p2, p3


def _augment_kv(k_ref, vt_ref, k_extra, kp_sc, vt_sc):
    seq = k_ref.shape[1]
    lane = _iota((seq, LANES), 1)
    for h in range(HEADS_IN_STEP):
        pr, hh = divmod(h, HEADS_PER_PAIR)
        hmask = (lane >= hh * HEAD_DIM) & (lane < (hh + 1) * HEAD_DIM)
        kp_sc[h] = jnp.where(hmask, k_ref[0, :, pr * LANES:(pr + 1) * LANES], k_extra)
        vt_sc[h, :HEAD_DIM, :] = vt_ref[h * HEAD_DIM:(h + 1) * HEAD_DIM, :]
        vt_sc[h, HEAD_DIM:, :] = jnp.ones((V_ROWS - HEAD_DIM, seq), vt_sc.dtype)


class _Unit:
    def __init__(self, h, i, qat, tile_term):
        self.h, self.i, self.qat, self.tile_term = h, i, qat, tile_term
        self.s_ref = self.mxp = self.m = self.acc = None


def _score_tile(u, j, kp_sc):
    key_in_tile = _iota((SLAB, BLOCK), 0)
    query_in_tile = _iota((SLAB, BLOCK), 1)
    s = jnp.dot(kp_sc[u.h, j * BLOCK:(j + 1) * BLOCK, :], u.qat, preferred_element_type=jnp.float32)
    for r in range(0, BLOCK, SLAB):
        piece = s[r:r + SLAB, :]
        if u.tile_term is not None:
            piece = piece + u.tile_term(j, r)
        if j == u.i:
            piece = jnp.where(key_in_tile + r <= query_in_tile, piece, NEG)
        u.s_ref[j * BLOCK + r:j * BLOCK + r + SLAB, :] = piece
        t = jnp.max(piece.reshape(SLAB // SUBLANES, SUBLANES, BLOCK), axis=0)
        u.mxp = t if u.mxp is None else jnp.maximum(u.mxp, t)


def _value_tile(u, j, vt_sc):
    if u.m is None:
        u.m = jnp.max(u.mxp, axis=0, keepdims=True)
    e = jnp.concatenate(
        [jnp.exp(u.s_ref[j * BLOCK + r:j * BLOCK + r + SLAB, :] - u.m).astype(jnp.bfloat16)
         for r in range(0, BLOCK, SLAB)], axis=0)
    pv = jnp.dot(vt_sc[u.h, :, j * BLOCK:(j + 1) * BLOCK], e, preferred_element_type=jnp.float32)
    u.acc = pv if u.acc is None else u.acc + pv


def _run_units(nb, make_unit, s_sc, kp_sc, vt_sc, o_ref):
    groups = [[(pr * HEADS_PER_PAIR + hh, i) for i in (a, nb - 1 - a) for hh in range(HEADS_PER_PAIR)]
              for pr in range(PAIRS_PER_STEP) for a in range(nb // 2)]
    prev = []
    for g, grp in enumerate(groups + [None]):
        cur = [make_unit(h, i) for h, i in grp] if grp is not None else []
        for n, u in enumerate(cur):
            u.s_ref = s_sc.at[(g % 2) * UNITS_PER_GROUP + n]
        for t in range(nb):
            for u in cur:
                if t <= u.i:
                    _score_tile(u, t, kp_sc)
            for u in prev:
                if t <= u.i:
                    _value_tile(u, t, vt_sc)
        for u in prev:
            out = u.acc[:HEAD_DIM, :] * (1.0 / u.acc[HEAD_DIM:HEAD_DIM + 1, :])
            o_ref[u.h * HEAD_DIM:(u.h + 1) * HEAD_DIM,
                  u.i * BLOCK:(u.i + 1) * BLOCK] = out.astype(o_ref.dtype)
        prev = cur


def _scaled_qt(qt_ref, h, i):
    pr = h // HEADS_PER_PAIR
    q = qt_ref[pr * LANES:(pr + 1) * LANES, i * BLOCK:(i + 1) * BLOCK]
    return q * jnp.asarray(SCALE, q.dtype)


def _with_spare_rows(qt, hh, spare):
    head = qt[hh * HEAD_DIM:(hh + 1) * HEAD_DIM, :]
    spare = jnp.concatenate(
        [spare, jnp.zeros((HEAD_DIM - spare.shape[0], BLOCK), qt.dtype)], axis=0)
    return jnp.concatenate([head, spare] if hh == 0 else [spare, head], axis=0)


def _attn_scratch(seq):
    return [pltpu.VMEM((2 * UNITS_PER_GROUP, seq, BLOCK), jnp.float32),
            pltpu.VMEM((HEADS_IN_STEP, seq, LANES), jnp.bfloat16),
            pltpu.VMEM((HEADS_IN_STEP, V_ROWS, seq), jnp.bfloat16)]


def _moba_kernel(qt_ref, k_ref, vt_ref, bias_ref, o_ref, s_sc, kp_sc, vt_sc, *, seq):
    nb = seq // BLOCK
    key_blk = _iota((seq, LANES), 0) // BLOCK
    onehot = ((_iota((seq, LANES), 1) % HEAD_DIM) == key_blk).astype(k_ref.dtype)
    _augment_kv(k_ref, vt_ref, onehot, kp_sc, vt_sc)
    blk_row = _iota((BF16_ROWS, BLOCK), 0)
    feat_row = _iota((LANES, BLOCK), 0)

    def pair_kmean(pr):
        k = k_ref[0, :, pr * LANES:(pr + 1) * LANES].astype(jnp.float32)
        kmean = jnp.sum(k.reshape(nb, BLOCK, LANES), axis=1) * (1.0 / BLOCK)
        return jnp.concatenate([kmean, jnp.zeros((BF16_ROWS - nb, LANES), jnp.float32)], axis=0)

    kmeans = [pair_kmean(pr) for pr in range(PAIRS_PER_STEP)]

    def make_unit(h, i):
        pr, hh = divmod(h, HEADS_PER_PAIR)
        qt = _scaled_qt(qt_ref, h, i)
        pen = jnp.zeros((BF16_ROWS, BLOCK), qt.dtype)
        if i > TOPK:
            in_head = (feat_row >= hh * HEAD_DIM) & (feat_row < (hh + 1) * HEAD_DIM)
            q_head = jnp.where(in_head, qt, jnp.zeros_like(qt)).astype(jnp.float32)
            gate = jnp.dot(kmeans[pr], q_head, precision=lax.Precision.HIGHEST,
                           preferred_element_type=jnp.float32)
            rank = jnp.zeros((BF16_ROWS, BLOCK), jnp.int32)
            for jp in range(i):
                other = gate[jp:jp + 1, :]
                beats = (other > gate) | ((other == gate) & (blk_row > jp))
                rank = rank + beats.astype(jnp.int32)
            pen = jnp.where((blk_row < i) & (rank >= TOPK), NEG, 0.0).astype(qt.dtype)

        def tile_term(j, r):
            off = (nb - 1 - i + j) * BLOCK + r
            return bias_ref[h, off:off + SLAB, :]

        return _Unit(h, i, _with_spare_rows(qt, hh, pen), tile_term)

    _run_units(nb, make_unit, s_sc, kp_sc, vt_sc, o_ref)


def _moba(k3, qvt, bias):
    b, seq, _ = k3.shape
    return pl.pallas_call(
        functools.partial(_moba_kernel, seq=seq),
        grid=(N_STEPS, b),
        in_specs=[
            pl.BlockSpec((STEP_WIDTH, seq), lambda p, bi: (p, bi)),
            pl.BlockSpec((1, seq, STEP_WIDTH), lambda p, bi: (bi, 0, p)),
            pl.BlockSpec((STEP_WIDTH, seq), lambda p, bi: (2 * N_STEPS + p, bi)),
            pl.BlockSpec((HEADS_IN_STEP, seq, BLOCK), lambda p, bi: (p, 0, 0)),
        ],
        out_specs=pl.BlockSpec((STEP_WIDTH, seq), lambda p, bi: (p, bi)),
        out_shape=jax.ShapeDtypeStruct((GROUP_WIDTH, b * seq), jnp.bfloat16),
        scratch_shapes=_attn_scratch(seq),
        compiler_params=pltpu.CompilerParams(
            dimension_semantics=("arbitrary", "arbitrary"), vmem_limit_bytes=VMEM_LIMIT),
        name="moba_attn",
    )(qvt, k3, qvt, bias)


def _fox_kernel(qt_ref, k_ref, vt_ref, logf_ref, o_ref, ext_sc, s_sc, kp_sc, vt_sc, *, seq):
    nb = seq // BLOCK
    p = pl.program_id(1)

    @pl.when(p == 0)
    def _():
        tri = (_iota((BLOCK, BLOCK), 1) <= _iota((BLOCK, BLOCK), 0)).astype(jnp.bfloat16)
        lane = _iota((BLOCK, LANES), 1) % HEAD_DIM
        carry = jnp.zeros((1, LANES), jnp.float32)
        for ch in range(nb):
            parts = _split3(logf_ref[ch * BLOCK:(ch + 1) * BLOCK, :])
            cs = jnp.dot(tri, jnp.concatenate(parts, axis=1), preferred_element_type=jnp.float32)
            c = cs[:, :LANES] + cs[:, LANES:2 * LANES] + cs[:, 2 * LANES:] + carry
            carry = c[BLOCK - 1:BLOCK, :]
            p1, p2, p3 = _split3(-c)
            part = lane % C_PARTS
            ext = jnp.where(part == 0, p1, jnp.where(part == 1, p2, p3))
            ext_sc[ch * BLOCK:(ch + 1) * BLOCK, :] = jnp.where(
                lane < C_PARTS * N_HEADS, ext, jnp.zeros_like(ext))

    _augment_kv(k_ref, vt_ref, ext_sc[...], kp_sc, vt_sc)
    n_spare = -(-C_PARTS * N_HEADS // BF16_ROWS) * BF16_ROWS
    row = _iota((n_spare, BLOCK), 0)

    def make_unit(h, i):
        qt = _scaled_qt(qt_ref, h, i)
        first = C_PARTS * (HEADS_IN_STEP * p + h)
        spare = jnp.where((row >= first) & (row < first + C_PARTS), 1.0, 0.0).astype(qt.dtype)
        return _Unit(h, i, _with_spare_rows(qt, h % HEADS_PER_PAIR, spare), None)

    _run_units(nb, make_unit, s_sc, kp_sc, vt_sc, o_ref)


def _fox(k3, qvt, logf3):
    b, seq, _ = k3.shape
    return pl.pallas_call(
        functools.partial(_fox_kernel, seq=seq),
        grid=(b, N_STEPS),
        in_specs=[
            pl.BlockSpec((STEP_WIDTH, seq), lambda bi, p: (N_STEPS + p, bi)),
            pl.BlockSpec((1, seq, STEP_WIDTH), lambda bi, p: (bi, 0, N_STEPS + p)),
            pl.BlockSpec((STEP_WIDTH, seq), lambda bi, p: (3 * N_STEPS + p, bi)),
            pl.BlockSpec((None, seq, LANES), lambda bi, p: (bi, 0, 0)),
        ],
        out_specs=pl.BlockSpec((STEP_WIDTH, seq), lambda bi, p: (p, bi)),
        out_shape=jax.ShapeDtypeStruct((GROUP_WIDTH, b * seq), jnp.bfloat16),
        scratch_shapes=[pltpu.VMEM((seq, LANES), jnp.bfloat16)] + _attn_scratch(seq),
        compiler_params=pltpu.CompilerParams(
            dimension_semantics=("arbitrary", "arbitrary"), vmem_limit_bytes=VMEM_LIMIT),
        name="fox_attn",
    )(qvt, k3, qvt, logf3)


def _dense_kernel(x_ref, ymt_ref, yft_ref, wo_ref, g_ref, wgu_ref, wd_ref, gfin_ref, o_ref, *, final):
    x1 = (x_ref[...]
          + lax.dot_general(ymt_ref[...], wo_ref[:GROUP_WIDTH, :], _TN,
                            preferred_element_type=jnp.float32)
          + lax.dot_general(yft_ref[...], wo_ref[GROUP_WIDTH:, :], _TN,
                            preferred_element_type=jnp.float32))
    h = _rms(x1, g_ref[...]).astype(jnp.bfloat16)
    acc = jnp.zeros_like(x1)
    for c in range(D_FF // FF_CHUNK):
        lo = c * FF_CHUNK
        gate = jnp.dot(h, wgu_ref[:, lo:lo + FF_CHUNK], preferred_element_type=jnp.float32)
        up = jnp.dot(h, wgu_ref[:, D_FF + lo:D_FF + lo + FF_CHUNK], preferred_element_type=jnp.float32)
        a = (gate / (1.0 + jnp.exp(-gate)) * up).astype(jnp.bfloat16)
        acc = acc + jnp.dot(a, wd_ref[lo:lo + FF_CHUNK, :], preferred_element_type=jnp.float32)
    acc = acc + x1
    if final:
        acc = _rms(acc, gfin_ref[...])
    o_ref[...] = acc


def _dense(x2d, ymt, yft, wo_all, g, wgu_all, wd_all, gfin, layer, final):
    t = x2d.shape[0]
    const = lambda shape: pl.BlockSpec(shape, lambda i: (0,) * len(shape),
                                       pipeline_mode=pl.Buffered(1))
    weights = lambda stacked: _layer_block(stacked, layer, pipeline_mode=pl.Buffered(1))
    return pl.pallas_call(
        functools.partial(_dense_kernel, final=final),
        grid=(t // TOKEN_TILE,),
        in_specs=[
            pl.BlockSpec((TOKEN_TILE, D_MODEL), lambda i: (i, 0)),
            pl.BlockSpec((GROUP_WIDTH, TOKEN_TILE), lambda i: (0, i)),
            pl.BlockSpec((GROUP_WIDTH, TOKEN_TILE), lambda i: (0, i)),
            weights(wo_all),
            const((1, D_MODEL)),
            weights(wgu_all),
            weights(wd_all),
            const((1, D_MODEL)),
        ],
        out_specs=pl.BlockSpec((TOKEN_TILE, D_MODEL), lambda i: (i, 0)),
        out_shape=jax.ShapeDtypeStruct((t, D_MODEL), jnp.float32),
        compiler_params=pltpu.CompilerParams(
            dimension_semantics=("arbitrary",), vmem_limit_bytes=VMEM_LIMIT),
        name="dense_ffn",
    )(x2d, ymt, yft, wo_all, g, wgu_all, wd_all, gfin)


def kernel(x, w_in, b_f, w_o, g_attn, w_gu, w_down, g_ffn, rel_bias, g_final):
    b, seq, d = x.shape
    depth = w_in.shape[0]
    assert d == D_MODEL and seq % BLOCK == 0 and (b * seq) % TOKEN_TILE == 0
    assert seq // BLOCK <= BF16_ROWS and C_PARTS * N_HEADS <= HEAD_DIM
    bf16 = jnp.bfloat16
    gw = GROUP_WIDTH
    bias = _bias_strips(rel_bias, seq)
    w_in16, w_o16, w_gu16, w_down16 = (w.astype(bf16) for w in (w_in, w_o, w_gu, w_down))
    w_f = _forget_lane_layout(w_in[:, :, 6 * gw:]).astype(bf16)
    b_lanes = _forget_lane_layout(b_f[:, None, :])
    x2d = x.reshape(b * seq, d)
    for layer in range(depth):
        k2, qvt, logf = _inproj(x2d, g_attn[layer][None, :], w_in16, layer, w_f[layer], b_lanes[layer])
        k3 = k2.reshape(b, seq, 2 * gw)
        ymt = _moba(k3, qvt, bias)
        yft = _fox(k3, qvt, logf.reshape(b, seq, LANES))
        x2d = _dense(x2d, ymt, yft, w_o16, g_ffn[layer][None, :], w_gu16, w_down16,
                     g_final[None, :], layer, final=(layer == depth - 1))
    return x2d.reshape(b, seq, d)
```

```python
import functools
import math

import jax
import jax.numpy as jnp
from jax import lax
from jax.experimental import pallas as pl
from jax.experimental.pallas import tpu as pltpu

D_MODEL = 1024
HEAD_DIM = 64
N_HEADS = 8
GROUP_WIDTH = N_HEADS * HEAD_DIM
BLOCK = 256
TOPK = 3
NUM_BUCKETS = 32
MAX_DISTANCE = 1024
D_FF = 2816
RMS_EPS = 1e-6
SCALE = HEAD_DIM ** -0.5
NEG = -1e30

LANES = 128
SUBLANES = 8
BF16_ROWS = 16
HEADS_PER_PAIR = LANES // HEAD_DIM
PAIRS_PER_STEP = 2
HEADS_IN_STEP = PAIRS_PER_STEP * HEADS_PER_PAIR
STEP_WIDTH = PAIRS_PER_STEP * LANES
N_STEPS = GROUP_WIDTH // STEP_WIDTH
UNITS_PER_GROUP = 2 * HEADS_PER_PAIR
C_PARTS = 3
V_ROWS = HEAD_DIM + BF16_ROWS
SLAB = 16
FF_CHUNK = 256
TOKEN_TILE = 1024
VMEM_LIMIT = 56 * 1024 * 1024

_NT = (((1,), (1,)), ((), ()))
_TN = (((0,), (0,)), ((), ()))


def _rms(x, g):
    return x * lax.rsqrt(jnp.mean(x * x, axis=-1, keepdims=True) + RMS_EPS) * g


def _inproj_kernel(x_ref, g_ref, w_ref, wf_ref, bf_ref, k_ref, qvt_ref, logf_ref):
    gw = GROUP_WIDTH
    h = _rms(x_ref[...], g_ref[...]).astype(jnp.bfloat16)

    def proj(group):
        return jnp.dot(h, w_ref[:, group * gw:(group + 1) * gw], preferred_element_type=jnp.float32)

    for n, group in enumerate((1, 4)):
        k_ref[:, n * gw:(n + 1) * gw] = proj(group).astype(jnp.bfloat16)
    for n, group in enumerate((0, 3, 2, 5)):
        qvt_ref[n * gw:(n + 1) * gw, :] = proj(group).T.astype(jnp.bfloat16)
    z = jnp.dot(h, wf_ref[...], preferred_element_type=jnp.float32) + bf_ref[...]
    logf_ref[...] = -(jnp.maximum(-z, 0.0) + jnp.log1p(jnp.exp(-jnp.abs(z))))


def _layer_block(stacked, layer, **kwargs):
    return pl.BlockSpec((None,) + stacked.shape[1:], lambda i: (layer, 0, 0), **kwargs)


def _inproj(x2d, g, w_all, layer, wf, bf):
    t = x2d.shape[0]
    const = lambda shape: pl.BlockSpec(shape, lambda i: (0, 0))
    return pl.pallas_call(
        _inproj_kernel,
        grid=(t // TOKEN_TILE,),
        in_specs=[
            pl.BlockSpec((TOKEN_TILE, D_MODEL), lambda i: (i, 0)),
            const((1, D_MODEL)),
            _layer_block(w_all, layer),
            const((D_MODEL, LANES)),
            const((1, LANES)),
        ],
        out_specs=[
            pl.BlockSpec((TOKEN_TILE, 2 * GROUP_WIDTH), lambda i: (i, 0)),
            pl.BlockSpec((4 * GROUP_WIDTH, TOKEN_TILE), lambda i: (0, i)),
            pl.BlockSpec((TOKEN_TILE, LANES), lambda i: (i, 0)),
        ],
        out_shape=[
            jax.ShapeDtypeStruct((t, 2 * GROUP_WIDTH), jnp.bfloat16),
            jax.ShapeDtypeStruct((4 * GROUP_WIDTH, t), jnp.bfloat16),
            jax.ShapeDtypeStruct((t, LANES), jnp.float32),
        ],
        compiler_params=pltpu.CompilerParams(
            dimension_semantics=("arbitrary",), vmem_limit_bytes=VMEM_LIMIT),
        name="inproj",
    )(x2d, g, w_all, wf, bf)


def _forget_lane_layout(cols):
    rep = jnp.repeat(cols, C_PARTS, axis=-1)
    half = jnp.pad(rep, [(0, 0)] * (cols.ndim - 1) + [(0, HEAD_DIM - C_PARTS * N_HEADS)])
    return jnp.concatenate([half, half], axis=-1)


def _bias_kernel(relt_ref, bkt_ref, out_ref, *, seq):
    h = pl.program_id(0)
    bkt = bkt_ref[...]
    w = jnp.zeros(bkt.shape, jnp.float32)
    for b in range(NUM_BUCKETS):
        w = jnp.where(bkt == b, relt_ref[h, b], w)
    x = jnp.broadcast_to(w, (BLOCK, bkt.shape[1]))
    y = pltpu.roll(x, 0, 1, stride=1, stride_axis=0)
    out_ref[0] = y[:, BLOCK:].T


def _t5_bucket(dist):
    max_exact = NUM_BUCKETS // 2
    is_small = dist < max_exact
    d = jnp.maximum(dist, 1).astype(jnp.float32)
    large = max_exact + (jnp.log(d / max_exact) / math.log(MAX_DISTANCE / max_exact)
                         * (NUM_BUCKETS - max_exact)).astype(jnp.int32)
    large = jnp.minimum(large, NUM_BUCKETS - 1)
    return jnp.where(is_small, dist, large)


def _bias_strips(rel_bias, seq):
    n = jnp.arange(seq + BLOCK, dtype=jnp.int32)
    bkt = _t5_bucket(jnp.clip(seq - n, 0, seq - 1))[None, :]
    return pl.pallas_call(
        functools.partial(_bias_kernel, seq=seq),
        grid=(N_HEADS,),
        in_specs=[
            pl.BlockSpec(memory_space=pltpu.SMEM),
            pl.BlockSpec((1, seq + BLOCK), lambda h: (0, 0)),
        ],
        out_specs=pl.BlockSpec((1, seq, BLOCK), lambda h: (h, 0, 0)),
        out_shape=jax.ShapeDtypeStruct((N_HEADS, seq, BLOCK), jnp.float32),
        compiler_params=pltpu.CompilerParams(dimension_semantics=("arbitrary",)),
        name="moba_bias",
    )(rel_bias.T, bkt)


def _iota(shape, axis):
    return lax.broadcasted_iota(jnp.int32, shape, axis)


def _split3(x):
    p1 = x.astype(jnp.bfloat16)
    r1 = x - p1.astype(jnp.float32)
    p2 = r1.astype(jnp.bfloat16)
    p3 = (r1 - p2.astype(jnp.float32)).astype(jnp.bfloat16)
    return p1, p2, p3


def _augment_kv(k_ref, vt_ref, k_extra, kp_sc, vt_sc):
    seq = k_ref.shape[1]
    lane = _iota((seq, LANES), 1)
    for h in range(HEADS_IN_STEP):
        pr, hh = divmod(h, HEADS_PER_PAIR)
        hmask = (lane >= hh * HEAD_DIM) & (lane < (hh + 1) * HEAD_DIM)
        kp_sc[h] = jnp.where(hmask, k_ref[0, :, pr * LANES:(pr + 1) * LANES], k_extra)
        vt_sc[h, :HEAD_DIM, :] = vt_ref[h * HEAD_DIM:(h + 1) * HEAD_DIM, :]
        vt_sc[h, HEAD_DIM:, :] = jnp.ones((V_ROWS - HEAD_DIM, seq), vt_sc.dtype)


class _Unit:
    def __init__(self, h, i, qat, tile_term):
        self.h, self.i, self.qat, self.tile_term = h, i, qat, tile_term
        self.s_ref = self.m = self.acc = None
        self.mxp = [None] * (BLOCK // LANES)


def _live_pieces(u, j, r):
    if j != u.i:
        return [(0, BLOCK, False)]
    return [(c0, c0 + LANES, r + SLAB - 1 > c0)
            for c0 in range(0, BLOCK, LANES) if r < c0 + LANES]


def _score_tile(u, j, kp_sc):
    s = jnp.dot(kp_sc[u.h, j * BLOCK:(j + 1) * BLOCK, :], u.qat, preferred_element_type=jnp.float32)
    for r in range(0, BLOCK, SLAB):
        for c0, c1, needs_mask in _live_pieces(u, j, r):
            piece = s[r:r + SLAB, c0:c1]
            if u.tile_term is not None:
                piece = piece + u.tile_term(j, r, c0, c1)
            if needs_mask:
                key = _iota(piece.shape, 0) + r
                query = _iota(piece.shape, 1) + c0
                piece = jnp.where(key <= query, piece, NEG)
            u.s_ref[j * BLOCK + r:j * BLOCK + r + SLAB, c0:c1] = piece
            t = jnp.max(piece.reshape(SLAB // SUBLANES, SUBLANES, c1 - c0), axis=0)
            for g in range(c0 // LANES, c1 // LANES):
                tg = t[:, g * LANES - c0:(g + 1) * LANES - c0]
                u.mxp[g] = tg if u.mxp[g] is None else jnp.maximum(u.mxp[g], tg)


def _value_tile(u, j, vt_sc):
    if u.m is None:
        u.m = jnp.concatenate([jnp.max(x, axis=0, keepdims=True) for x in u.mxp], axis=1)
    slabs = []
    for r in range(0, BLOCK, SLAB):
        live = {c0: c1 for c0, c1, _ in _live_pieces(u, j, r)}
        parts = []
        c0 = 0
        while c0 < BLOCK:
            if c0 in live:
                c1 = live[c0]
                parts.append(jnp.exp(u.s_ref[j * BLOCK + r:j * BLOCK + r + SLAB, c0:c1]
                                     - u.m[:, c0:c1]).astype(jnp.bfloat16))
            else:
                c1 = c0 + LANES
                parts.append(jnp.zeros((SLAB, LANES), jnp.bfloat16))
            c0 = c1
        slabs.append(parts[0] if len(parts) == 1 else jnp.concatenate(parts, axis=1))
    e = jnp.concatenate(slabs, axis=0)
    pv = jnp.dot(vt_sc[u.h, :, j * BLOCK:(j + 1) * BLOCK], e, preferred_element_type=jnp.float32)
    u.acc = pv if u.acc is None else u.acc + pv


def _run_units(nb, make_unit, s_sc, kp_sc, vt_sc, o_ref):
    groups = [[(pr * HEADS_PER_PAIR + hh, i) for i in (a, nb - 1 - a) for hh in range(HEADS_PER_PAIR)]
              for pr in range(PAIRS_PER_STEP) for a in range(nb // 2)]
    prev = []
    for g, grp in enumerate(groups + [None]):
        cur = [make_unit(h, i) for h, i in grp] if grp is not None else []
        for n, u in enumerate(cur):
            u.s_ref = s_sc.at[(g % 2) * UNITS_PER_GROUP + n]
        for t in range(nb):
            for u in cur:
                if t <= u.i:
                    _score_tile(u, t, kp_sc)
            for u in prev:
                if t <= u.i:
                    _value_tile(u, t, vt_sc)
        for u in prev:
            out = u.acc[:HEAD_DIM, :] * (1.0 / u.acc[HEAD_DIM:HEAD_DIM + 1, :])
            o_ref[u.h * HEAD_DIM:(u.h + 1) * HEAD_DIM,
                  u.i * BLOCK:(u.i + 1) * BLOCK] = out.astype(o_ref.dtype)
        prev = cur


def _scaled_qt(qt_ref, h, i):
    pr = h // HEADS_PER_PAIR
    q = qt_ref[pr * LANES:(pr + 1) * LANES, i * BLOCK:(i + 1) * BLOCK]
    return q * jnp.asarray(SCALE, q.dtype)


def _with_spare_rows(qt, hh, spare):
    head = qt[hh * HEAD_DIM:(hh + 1) * HEAD_DIM, :]
    spare = jnp.concatenate(
        [spare, jnp.zeros((HEAD_DIM - spare.shape[0], BLOCK), qt.dtype)], axis=0)
    return jnp.concatenate([head, spare] if hh == 0 else [spare, head], axis=0)


def _attn_scratch(seq):
    return [pltpu.VMEM((2 * UNITS_PER_GROUP, seq, BLOCK), jnp.float32),
            pltpu.VMEM((HEADS_IN_STEP, seq, LANES), jnp.bfloat16),
            pltpu.VMEM((HEADS_IN_STEP, V_ROWS, seq), jnp.bfloat16)]


def _moba_kernel(qt_ref, k_ref, vt_ref, bias_ref, o_ref, s_sc, kp_sc, vt_sc, *, seq):
    nb = seq // BLOCK
    key_blk = _iota((seq, LANES), 0) // BLOCK
    onehot = ((_iota((seq, LANES), 1) % HEAD_DIM) == key_blk).astype(k_ref.dtype)
    _augment_kv(k_ref, vt_ref, onehot, kp_sc, vt_sc)
    blk_row = _iota((BF16_ROWS, BLOCK), 0)
    feat_row = _iota((LANES, BLOCK), 0)

    def pair_kmean(pr):
        k = k_ref[0, :, pr * LANES:(pr + 1) * LANES].astype(jnp.float32)
        kmean = jnp.sum(k.reshape(nb, BLOCK, LANES), axis=1) * (1.0 / BLOCK)
        return jnp.concatenate([kmean, jnp.zeros((BF16_ROWS - nb, LANES), jnp.float32)], axis=0)

    kmeans = [pair_kmean(pr) for pr in range(PAIRS_PER_STEP)]

    def make_unit(h, i):
        pr, hh = divmod(h, HEADS_PER_PAIR)
        qt = _scaled_qt(qt_ref, h, i)
        pen = jnp.zeros((BF16_ROWS, BLOCK), qt.dtype)
        if i > TOPK:
            in_head = (feat_row >= hh * HEAD_DIM) & (feat_row < (hh + 1) * HEAD_DIM)
            q_head = jnp.where(in_head, qt, jnp.zeros_like(qt)).astype(jnp.float32)
            gate = jnp.dot(kmeans[pr], q_head, precision=lax.Precision.HIGHEST,
                           preferred_element_type=jnp.float32)
            rank = jnp.zeros((BF16_ROWS, BLOCK), jnp.int32)
            for jp in range(i):
                other = gate[jp:jp + 1, :]
                beats = (other > gate) | ((other == gate) & (blk_row > jp))
                rank = rank + beats.astype(jnp.int32)
            pen = jnp.where((blk_row < i) & (rank >= TOPK), NEG, 0.0).astype(qt.dtype)

        def tile_term(j, r, c0, c1):
            off = (nb - 1 - i + j) * BLOCK + r
            return bias_ref[h, off:off + SLAB, c0:c1]

        return _Unit(h, i, _with_spare_rows(qt, hh, pen), tile_term)

    _run_units(nb, make_unit, s_sc, kp_sc, vt_sc, o_ref)


def _moba(k3, qvt, bias):
    b, seq, _ = k3.shape
    return pl.pallas_call(
        functools.partial(_moba_kernel, seq=seq),
        grid=(N_STEPS, b),
        in_specs=[
            pl.BlockSpec((STEP_WIDTH, seq), lambda p, bi: (p, bi)),
            pl.BlockSpec((1, seq, STEP_WIDTH), lambda p, bi: (bi, 0, p)),
            pl.BlockSpec((STEP_WIDTH, seq), lambda p, bi: (2 * N_STEPS + p, bi)),
            pl.BlockSpec((HEADS_IN_STEP, seq, BLOCK), lambda p, bi: (p, 0, 0)),
        ],
        out_specs=pl.BlockSpec((STEP_WIDTH, seq), lambda p, bi: (p, bi)),
        out_shape=jax.ShapeDtypeStruct((GROUP_WIDTH, b * seq), jnp.bfloat16),
        scratch_shapes=_attn_scratch(seq),
        compiler_params=pltpu.CompilerParams(
            dimension_semantics=("arbitrary", "arbitrary"), vmem_limit_bytes=VMEM_LIMIT),
        name="moba_attn",
    )(qvt, k3, qvt, bias)


def _fox_kernel(qt_ref, k_ref, vt_ref, logf_ref, o_ref, ext_sc, s_sc, kp_sc, vt_sc, *, seq):
    nb = seq // BLOCK
    p = pl.program_id(1)

    @pl.when(p == 0)
    def _():
        tri = (_iota((BLOCK, BLOCK), 1) <= _iota((BLOCK, BLOCK), 0)).astype(jnp.bfloat16)
        lane = _iota((BLOCK, LANES), 1) % HEAD_DIM
        carry = jnp.zeros((1, LANES), jnp.float32)
        for ch in range(nb):
            parts = _split3(logf_ref[ch * BLOCK:(ch + 1) * BLOCK, :])
            cs = jnp.dot(tri, jnp.concatenate(parts, axis=1), preferred_element_type=jnp.float32)
            c = cs[:, :LANES] + cs[:, LANES:2 * LANES] + cs[:, 2 * LANES:] + carry
            carry = c[BLOCK - 1:BLOCK, :]
            p1, p2, p3 = _split3(-c)
            part = lane % C_PARTS
            ext = jnp.where(part == 0, p1, jnp.where(part == 1, p2, p3))
            ext_sc[ch * BLOCK:(ch + 1) * BLOCK, :] = jnp.where(
                lane < C_PARTS * N_HEADS, ext, jnp.zeros_like(ext))

    _augment_kv(k_ref, vt_ref, ext_sc[...], kp_sc, vt_sc)
    n_spare = -(-C_PARTS * N_HEADS // BF16_ROWS) * BF16_ROWS
    row = _iota((n_spare, BLOCK), 0)

    def make_unit(h, i):
        qt = _scaled_qt(qt_ref, h, i)
        first = C_PARTS * (HEADS_IN_STEP * p + h)
        spare = jnp.where((row >= first) & (row < first + C_PARTS), 1.0, 0.0).astype(qt.dtype)
        return _Unit(h, i, _with_spare_rows(qt, h % HEADS_PER_PAIR, spare), None)

    _run_units(nb, make_unit, s_sc, kp_sc, vt_sc, o_ref)


def _fox(k3, qvt, logf3):
    b, seq, _ = k3.shape
    return pl.pallas_call(
        functools.partial(_fox_kernel, seq=seq),
        grid=(b, N_STEPS),
        in_specs=[
            pl.BlockSpec((STEP_WIDTH, seq), lambda bi, p: (N_STEPS + p, bi)),
            pl.BlockSpec((1, seq, STEP_WIDTH), lambda bi, p: (bi, 0, N_STEPS + p)),
            pl.BlockSpec((STEP_WIDTH, seq), lambda bi, p: (3 * N_STEPS + p, bi)),
            pl.BlockSpec((None, seq, LANES), lambda bi, p: (bi, 0, 0)),
        ],
        out_specs=pl.BlockSpec((STEP_WIDTH, seq), lambda bi, p: (p, bi)),
        out_shape=jax.ShapeDtypeStruct((GROUP_WIDTH, b * seq), jnp.bfloat16),
        scratch_shapes=[pltpu.VMEM((seq, LANES), jnp.bfloat16)] + _attn_scratch(seq),
        compiler_params=pltpu.CompilerParams(
            dimension_semantics=("arbitrary", "arbitrary"), vmem_limit_bytes=VMEM_LIMIT),
        name="fox_attn",
    )(qvt, k3, qvt, logf3)


def _dense_kernel(x_ref, ymt_ref, yft_ref, wo_ref, g_ref, wgu_ref, wd_ref, gfin_ref, o_ref, *, final):
    x1 = (x_ref[...]
          + lax.dot_general(ymt_ref[...], wo_ref[:GROUP_WIDTH, :], _TN,
                            preferred_element_type=jnp.float32)
          + lax.dot_general(yft_ref[...], wo_ref[GROUP_WIDTH:, :], _TN,
                            preferred_element_type=jnp.float32))
    h = _rms(x1, g_ref[...]).astype(jnp.bfloat16)
    acc = jnp.zeros_like(x1)
    for c in range(D_FF // FF_CHUNK):
        lo = c * FF_CHUNK
        gate = jnp.dot(h, wgu_ref[:, lo:lo + FF_CHUNK], preferred_element_type=jnp.float32)
        up = jnp.dot(h, wgu_ref[:, D_FF + lo:D_FF + lo + FF_CHUNK], preferred_element_type=jnp.float32)
        a = (gate / (1.0 + jnp.exp(-gate)) * up).astype(jnp.bfloat16)
        acc = acc + jnp.dot(a, wd_ref[lo:lo + FF_CHUNK, :], preferred_element_type=jnp.float32)
    acc = acc + x1
    if final:
        acc = _rms(acc, gfin_ref[...])
    o_ref[...] = acc


def _dense(x2d, ymt, yft, wo_all, g, wgu_all, wd_all, gfin, layer, final):
    t = x2d.shape[0]
    const = lambda shape: pl.BlockSpec(shape, lambda i: (0,) * len(shape),
                                       pipeline_mode=pl.Buffered(1))
    weights = lambda stacked: _layer_block(stacked, layer, pipeline_mode=pl.Buffered(1))
    return pl.pallas_call(
        functools.partial(_dense_kernel, final=final),
        grid=(t // TOKEN_TILE,),
        in_specs=[
            pl.BlockSpec((TOKEN_TILE, D_MODEL), lambda i: (i, 0)),
            pl.BlockSpec((GROUP_WIDTH, TOKEN_TILE), lambda i: (0, i)),
            pl.BlockSpec((GROUP_WIDTH, TOKEN_TILE), lambda i: (0, i)),
            weights(wo_all),
            const((1, D_MODEL)),
            weights(wgu_all),
            weights(wd_all),
            const((1, D_MODEL)),
        ],
        out_specs=pl.BlockSpec((TOKEN_TILE, D_MODEL), lambda i: (i, 0)),
        out_shape=jax.ShapeDtypeStruct((t, D_MODEL), jnp.float32),
        compiler_params=pltpu.CompilerParams(
            dimension_semantics=("arbitrary",), vmem_limit_bytes=VMEM_LIMIT),
        name="dense_ffn",
    )(x2d, ymt, yft, wo_all, g, wgu_all, wd_all, gfin)


def kernel(x, w_in, b_f, w_o, g_attn, w_gu, w_down, g_ffn, rel_bias, g_final):
    b, seq, d = x.shape
    depth = w_in.shape[0]
    assert d == D_MODEL and seq % BLOCK == 0 and (b * seq) % TOKEN_TILE == 0
    assert seq // BLOCK <= BF16_ROWS and C_PARTS * N_HEADS <= HEAD_DIM
    bf16 = jnp.bfloat16
    gw = GROUP_WIDTH
    bias = _bias_strips(rel_bias, seq)
    w_in16, w_o16, w_gu16, w_down16 = (w.astype(bf16) for w in (w_in, w_o, w_gu, w_down))
    w_f = _forget_lane_layout(w_in[:, :, 6 * gw:]).astype(bf16)
    b_lanes = _forget_lane_layout(b_f[:, None, :])
    x2d = x.reshape(b * seq, d)
    for layer in range(depth):
        k2, qvt, logf = _inproj(x2d, g_attn[layer][None, :], w_in16, layer, w_f[layer], b_lanes[layer])
        k3 = k2.reshape(b, seq, 2 * gw)
        ymt = _moba(k3, qvt, bias)
        yft = _fox(k3, qvt, logf.reshape(b, seq, LANES))
        x2d = _dense(x2d, ymt, yft, w_o16, g_ffn[layer][None, :], w_gu16, w_down16,
                     g_final[None, :], layer, final=(layer == depth - 1))
    return x2d.reshape(b, seq, d)
```

```python
import functools
import math

import jax
import jax.numpy as jnp
from jax import lax
from jax.experimental import pallas as pl
from jax.experimental.pallas import tpu as pltpu

D_MODEL = 1024
HEAD_DIM = 64
N_HEADS = 8
GROUP_WIDTH = N_HEADS * HEAD_DIM
BLOCK = 256
TOPK = 3
NUM_BUCKETS = 32
MAX_DISTANCE = 1024
D_FF = 2816
RMS_EPS = 1e-6
SCALE = HEAD_DIM ** -0.5
NEG = -1e30

LANES = 128
SUBLANES = 8
BF16_ROWS = 16
HEADS_PER_PAIR = LANES // HEAD_DIM
PAIRS_PER_STEP = 2
HEADS_IN_STEP = PAIRS_PER_STEP * HEADS_PER_PAIR
STEP_WIDTH = PAIRS_PER_STEP * LANES
N_STEPS = GROUP_WIDTH // STEP_WIDTH
UNITS_PER_GROUP = 2 * HEADS_PER_PAIR
C_PARTS = 3
V_ROWS = HEAD_DIM + BF16_ROWS
SLAB = 16
FF_CHUNK = 256
TOKEN_TILE = 1024
VMEM_LIMIT = 56 * 1024 * 1024

_NT = (((1,), (1,)), ((), ()))
_TN = (((0,), (0,)), ((), ()))


def _rms(x, g):
    return x * lax.rsqrt(jnp.mean(x * x, axis=-1, keepdims=True) + RMS_EPS) * g


def _inproj_kernel(x_ref, g_ref, w_ref, wf_ref, bf_ref, k_ref, qvt_ref, logf_ref):
    gw = GROUP_WIDTH
    h = _rms(x_ref[...], g_ref[...]).astype(jnp.bfloat16)

    def proj(group):
        return jnp.dot(h, w_ref[:, group * gw:(group + 1) * gw], preferred_element_type=jnp.float32)

    for n, group in enumerate((1, 4)):
        k_ref[:, n * gw:(n + 1) * gw] = proj(group).astype(jnp.bfloat16)
    for n, group in enumerate((0, 3, 2, 5)):
        qvt_ref[n * gw:(n + 1) * gw, :] = proj(group).T.astype(jnp.bfloat16)
    z = jnp.dot(h, wf_ref[...], preferred_element_type=jnp.float32) + bf_ref[...]
    logf_ref[...] = -(jnp.maximum(-z, 0.0) + jnp.log1p(jnp.exp(-jnp.abs(z))))


def _layer_block(stacked, layer, **kwargs):
    return pl.BlockSpec((None,) + stacked.shape[1:], lambda i: (layer, 0, 0), **kwargs)


def _inproj(x2d, g, w_all, layer, wf, bf):
    t = x2d.shape[0]
    const = lambda shape: pl.BlockSpec(shape, lambda i: (0, 0))
    return pl.pallas_call(
        _inproj_kernel,
        grid=(t // TOKEN_TILE,),
        in_specs=[
            pl.BlockSpec((TOKEN_TILE, D_MODEL), lambda i: (i, 0)),
            const((1, D_MODEL)),
            _layer_block(w_all, layer),
            const((D_MODEL, LANES)),
            const((1, LANES)),
        ],
        out_specs=[
            pl.BlockSpec((TOKEN_TILE, 2 * GROUP_WIDTH), lambda i: (i, 0)),
            pl.BlockSpec((4 * GROUP_WIDTH, TOKEN_TILE), lambda i: (0, i)),
            pl.BlockSpec((TOKEN_TILE, LANES), lambda i: (i, 0)),
        ],
        out_shape=[
            jax.ShapeDtypeStruct((t, 2 * GROUP_WIDTH), jnp.bfloat16),
            jax.ShapeDtypeStruct((4 * GROUP_WIDTH, t), jnp.bfloat16),
            jax.ShapeDtypeStruct((t, LANES), jnp.float32),
        ],
        compiler_params=pltpu.CompilerParams(
            dimension_semantics=("arbitrary",), vmem_limit_bytes=VMEM_LIMIT),
        name="inproj",
    )(x2d, g, w_all, wf, bf)


def _forget_lane_layout(cols):
    rep = jnp.repeat(cols, C_PARTS, axis=-1)
    half = jnp.pad(rep, [(0, 0)] * (cols.ndim - 1) + [(0, HEAD_DIM - C_PARTS * N_HEADS)])
    return jnp.concatenate([half, half], axis=-1)


def _bias_kernel(relt_ref, bkt_ref, out_ref, *, seq):
    h = pl.program_id(0)
    bkt = bkt_ref[...]
    w = jnp.zeros(bkt.shape, jnp.float32)
    for b in range(NUM_BUCKETS):
        w = jnp.where(bkt == b, relt_ref[h, b], w)
    x = jnp.broadcast_to(w, (BLOCK, bkt.shape[1]))
    y = pltpu.roll(x, 0, 1, stride=1, stride_axis=0)
    out_ref[0] = y[:, BLOCK:].T


def _t5_bucket(dist):
    max_exact = NUM_BUCKETS // 2
    is_small = dist < max_exact
    d = jnp.maximum(dist, 1).astype(jnp.float32)
    large = max_exact + (jnp.log(d / max_exact) / math.log(MAX_DISTANCE / max_exact)
                         * (NUM_BUCKETS - max_exact)).astype(jnp.int32)
    large = jnp.minimum(large, NUM_BUCKETS - 1)
    return jnp.where(is_small, dist, large)


def _bias_strips(rel_bias, seq):
    n = jnp.arange(seq + BLOCK, dtype=jnp.int32)
    bkt = _t5_bucket(jnp.clip(seq - n, 0, seq - 1))[None, :]
    return pl.pallas_call(
        functools.partial(_bias_kernel, seq=seq),
        grid=(N_HEADS,),
        in_specs=[
            pl.BlockSpec(memory_space=pltpu.SMEM),
            pl.BlockSpec((1, seq + BLOCK), lambda h: (0, 0)),
        ],
        out_specs=pl.BlockSpec((1, seq, BLOCK), lambda h: (h, 0, 0)),
        out_shape=jax.ShapeDtypeStruct((N_HEADS, seq, BLOCK), jnp.float32),
        compiler_params=pltpu.CompilerParams(dimension_semantics=("arbitrary",)),
        name="moba_bias",
    )(rel_bias.T, bkt)


def _iota(shape, axis):
    return lax.broadcasted_iota(jnp.int32, shape, axis)


def _split3(x):
    p1 = x.astype(jnp.bfloat16)
    r1 = x - p1.astype(jnp.float32)
    p2 = r1.astype(jnp.bfloat16)
    p3 = (r1 - p2.astype(jnp.float32)).astype(jnp.bfloat16)
    return p1, p2, p3


def _augment_kv(k_ref, vt_ref, k_extra, kp_sc, vt_sc):
    seq = k_ref.shape[1]
    lane = _iota((seq, LANES), 1)
    for h in range(HEADS_IN_STEP):
        pr, hh = divmod(h, HEADS_PER_PAIR)
        hmask = (lane >= hh * HEAD_DIM) & (lane < (hh + 1) * HEAD_DIM)
        kp_sc[h] = jnp.where(hmask, k_ref[0, :, pr * LANES:(pr + 1) * LANES], k_extra)
        vt_sc[h, :HEAD_DIM, :] = vt_ref[h * HEAD_DIM:(h + 1) * HEAD_DIM, :]
        vt_sc[h, HEAD_DIM:, :] = jnp.ones((V_ROWS - HEAD_DIM, seq), vt_sc.dtype)


class _Unit:
    def __init__(self, h, i, qat, tile_term):
        self.h, self.i, self.qat, self.tile_term = h, i, qat, tile_term
        self.s_ref = self.mxp = self.m = self.acc = None


def _score_tile(u, j, kp_sc):
    key_in_tile = _iota((SLAB, BLOCK), 0)
    query_in_tile = _iota((SLAB, BLOCK), 1)
    s = jnp.dot(kp_sc[u.h, j * BLOCK:(j + 1) * BLOCK, :], u.qat, preferred_element_type=jnp.float32)
    for r in range(0, BLOCK, SLAB):
        piece = s[r:r + SLAB, :]
        if u.tile_term is not None:
            piece = piece + u.tile_term(j, r)
        if j == u.i:
            piece = jnp.where(key_in_tile + r <= query_in_tile, piece, NEG)
        u.s_ref[j * BLOCK + r:j * BLOCK + r + SLAB, :] = piece
        t = jnp.max(piece.reshape(SLAB // SUBLANES, SUBLANES, BLOCK), axis=0)
        u.mxp = t if u.mxp is None else jnp.maximum(u.mxp, t)


def _value_tile(u, j, vt_sc):
    if u.m is None:
        u.m = jnp.max(u.mxp, axis=0, keepdims=True)
    e = jnp.concatenate(
        [jnp.exp(u.s_ref[j * BLOCK + r:j * BLOCK + r + SLAB, :] - u.m).astype(jnp.bfloat16)
         for r in range(0, BLOCK, SLAB)], axis=0)
    pv = jnp.dot(vt_sc[u.h, :, j * BLOCK:(j + 1) * BLOCK], e, preferred_element_type=jnp.float32)
    u.acc = pv if u.acc is None else u.acc + pv


def _run_units(nb, make_unit, s_sc, kp_sc, vt_sc, o_ref):
    groups = [[(pr * HEADS_PER_PAIR + hh, i) for i in (a, nb - 1 - a) for hh in range(HEADS_PER_PAIR)]
              for pr in range(PAIRS_PER_STEP) for a in range(nb // 2)]
    prev = []
    for g, grp in enumerate(groups + [None]):
        cur = [make_unit(h, i) for h, i in grp] if grp is not None else []
        for n, u in enumerate(cur):
            u.s_ref = s_sc.at[(g % 2) * UNITS_PER_GROUP + n]
        s_items = [(u, t) for u in cur for t in range(u.i + 1)]
        v_items = [(u, t) for t in range(nb) for u in prev if t <= u.i]
        for n in range(max(len(s_items), len(v_items))):
            if n < len(s_items):
                _score_tile(*s_items[n], kp_sc)
            if n < len(v_items):
                _value_tile(*v_items[n], vt_sc)
        for u in prev:
            out = u.acc[:HEAD_DIM, :] * (1.0 / u.acc[HEAD_DIM:HEAD_DIM + 1, :])
            o_ref[u.h * HEAD_DIM:(u.h + 1) * HEAD_DIM,
                  u.i * BLOCK:(u.i + 1) * BLOCK] = out.astype(o_ref.dtype)
        prev = cur


def _scaled_qt(qt_ref, h, i):
    pr = h // HEADS_PER_PAIR
    q = qt_ref[pr * LANES:(pr + 1) * LANES, i * BLOCK:(i + 1) * BLOCK]
    return q * jnp.asarray(SCALE, q.dtype)


def _with_spare_rows(qt, hh, spare):
    head = qt[hh * HEAD_DIM:(hh + 1) * HEAD_DIM, :]
    spare = jnp.concatenate(
        [spare, jnp.zeros((HEAD_DIM - spare.shape[0], BLOCK), qt.dtype)], axis=0)
    return jnp.concatenate([head, spare] if hh == 0 else [spare, head], axis=0)


def _attn_scratch(seq):
    return [pltpu.VMEM((2 * UNITS_PER_GROUP, seq, BLOCK), jnp.float32),
            pltpu.VMEM((HEADS_IN_STEP, seq, LANES), jnp.bfloat16),
            pltpu.VMEM((HEADS_IN_STEP, V_ROWS, seq), jnp.bfloat16)]


def _moba_kernel(qt_ref, k_ref, vt_ref, bias_ref, o_ref, s_sc, kp_sc, vt_sc, *, seq):
    nb = seq // BLOCK
    key_blk = _iota((seq, LANES), 0) // BLOCK
    onehot = ((_iota((seq, LANES), 1) % HEAD_DIM) == key_blk).astype(k_ref.dtype)
    _augment_kv(k_ref, vt_ref, onehot, kp_sc, vt_sc)
    blk_row = _iota((BF16_ROWS, BLOCK), 0)
    feat_row = _iota((LANES, BLOCK), 0)

    def pair_kmean(pr):
        k = k_ref[0, :, pr * LANES:(pr + 1) * LANES].astype(jnp.float32)
        kmean = jnp.sum(k.reshape(nb, BLOCK, LANES), axis=1) * (1.0 / BLOCK)
        return jnp.concatenate([kmean, jnp.zeros((BF16_ROWS - nb, LANES), jnp.float32)], axis=0)

    kmeans = [pair_kmean(pr) for pr in range(PAIRS_PER_STEP)]

    def make_unit(h, i):
        pr, hh = divmod(h, HEADS_PER_PAIR)
        qt = _scaled_qt(qt_ref, h, i)
        pen = jnp.zeros((BF16_ROWS, BLOCK), qt.dtype)
        if i > TOPK:
            in_head = (feat_row >= hh * HEAD_DIM) & (feat_row < (hh + 1) * HEAD_DIM)
            q_head = jnp.where(in_head, qt, jnp.zeros_like(qt)).astype(jnp.float32)
            gate = jnp.dot(kmeans[pr], q_head, precision=lax.Precision.HIGHEST,
                           preferred_element_type=jnp.float32)
            rank = jnp.zeros((BF16_ROWS, BLOCK), jnp.int32)
            for jp in range(i):
                other = gate[jp:jp + 1, :]
                beats = (other > gate) | ((other == gate) & (blk_row > jp))
                rank = rank + beats.astype(jnp.int32)
            pen = jnp.where((blk_row < i) & (rank >= TOPK), NEG, 0.0).astype(qt.dtype)

        def tile_term(j, r):
            off = (nb - 1 - i + j) * BLOCK + r
            return bias_ref[h, off:off + SLAB, :]

        return _Unit(h, i, _with_spare_rows(qt, hh, pen), tile_term)

    _run_units(nb, make_unit, s_sc, kp_sc, vt_sc, o_ref)


def _moba(k3, qvt, bias):
    b, seq, _ = k3.shape
    return pl.pallas_call(
        functools.partial(_moba_kernel, seq=seq),
        grid=(N_STEPS, b),
        in_specs=[
            pl.BlockSpec((STEP_WIDTH, seq), lambda p, bi: (p, bi)),
            pl.BlockSpec((1, seq, STEP_WIDTH), lambda p, bi: (bi, 0, p)),
            pl.BlockSpec((STEP_WIDTH, seq), lambda p, bi: (2 * N_STEPS + p, bi)),
            pl.BlockSpec((HEADS_IN_STEP, seq, BLOCK), lambda p, bi: (p, 0, 0)),
        ],
        out_specs=pl.BlockSpec((STEP_WIDTH, seq), lambda p, bi: (p, bi)),
        out_shape=jax.ShapeDtypeStruct((GROUP_WIDTH, b * seq), jnp.bfloat16),
        scratch_shapes=_attn_scratch(seq),
        compiler_params=pltpu.CompilerParams(
            dimension_semantics=("arbitrary", "arbitrary"), vmem_limit_bytes=VMEM_LIMIT),
        name="moba_attn",
    )(qvt, k3, qvt, bias)


def _fox_kernel(qt_ref, k_ref, vt_ref, logf_ref, o_ref, ext_sc, s_sc, kp_sc, vt_sc, *, seq):
    nb = seq // BLOCK
    p = pl.program_id(1)

    @pl.when(p == 0)
    def _():
        tri = (_iota((BLOCK, BLOCK), 1) <= _iota((BLOCK, BLOCK), 0)).astype(jnp.bfloat16)
        lane = _iota((BLOCK, LANES), 1) % HEAD_DIM
        carry = jnp.zeros((1, LANES), jnp.float32)
        for ch in range(nb):
            parts = _split3(logf_ref[ch * BLOCK:(ch + 1) * BLOCK, :])
            cs = jnp.dot(tri, jnp.concatenate(parts, axis=1), preferred_element_type=jnp.float32)
            c = cs[:, :LANES] + cs[:, LANES:2 * LANES] + cs[:, 2 * LANES:] + carry
            carry = c[BLOCK - 1:BLOCK, :]
            p1, p2, p3 = _split3(-c)
            part = lane % C_PARTS
            ext = jnp.where(part == 0, p1, jnp.where(part == 1, p2, p3))
            ext_sc[ch * BLOCK:(ch + 1) * BLOCK, :] = jnp.where(
                lane < C_PARTS * N_HEADS, ext, jnp.zeros_like(ext))

    _augment_kv(k_ref, vt_ref, ext_sc[...], kp_sc, vt_sc)
    n_spare = -(-C_PARTS * N_HEADS // BF16_ROWS) * BF16_ROWS
    row = _iota((n_spare, BLOCK), 0)

    def make_unit(h, i):
        qt = _scaled_qt(qt_ref, h, i)
        first = C_PARTS * (HEADS_IN_STEP * p + h)
        spare = jnp.where((row >= first) & (row < first + C_PARTS), 1.0, 0.0).astype(qt.dtype)
        return _Unit(h, i, _with_spare_rows(qt, h % HEADS_PER_PAIR, spare), None)

    _run_units(nb, make_unit, s_sc, kp_sc, vt_sc, o_ref)


def _fox(k3, qvt, logf3):
    b, seq, _ = k3.shape
    return pl.pallas_call(
        functools.partial(_fox_kernel, seq=seq),
        grid=(b, N_STEPS),
        in_specs=[
            pl.BlockSpec((STEP_WIDTH, seq), lambda bi, p: (N_STEPS + p, bi)),
            pl.BlockSpec((1, seq, STEP_WIDTH), lambda bi, p: (bi, 0, N_STEPS + p)),
            pl.BlockSpec((STEP_WIDTH, seq), lambda bi, p: (3 * N_STEPS + p, bi)),
            pl.BlockSpec((None, seq, LANES), lambda bi, p: (bi, 0, 0)),
        ],
        out_specs=pl.BlockSpec((STEP_WIDTH, seq), lambda bi, p: (p, bi)),
        out_shape=jax.ShapeDtypeStruct((GROUP_WIDTH, b * seq), jnp.bfloat16),
        scratch_shapes=[pltpu.VMEM((seq, LANES), jnp.bfloat16)] + _attn_scratch(seq),
        compiler_params=pltpu.CompilerParams(
            dimension_semantics=("arbitrary", "arbitrary"), vmem_limit_bytes=VMEM_LIMIT),
        name="fox_attn",
    )(qvt, k3, qvt, logf3)


def _dense_kernel(x_ref, ymt_ref, yft_ref, wo_ref, g_ref, wgu_ref, wd_ref, gfin_ref, o_ref, *, final):
    x1 = (x_ref[...]
          + lax.dot_general(ymt_ref[...], wo_ref[:GROUP_WIDTH, :], _TN,
                            preferred_element_type=jnp.float32)
          + lax.dot_general(yft_ref[...], wo_ref[GROUP_WIDTH:, :], _TN,
                            preferred_element_type=jnp.float32))
    h = _rms(x1, g_ref[...]).astype(jnp.bfloat16)
    acc = jnp.zeros_like(x1)
    for c in range(D_FF // FF_CHUNK):
        lo = c * FF_CHUNK
        gate = jnp.dot(h, wgu_ref[:, lo:lo + FF_CHUNK], preferred_element_type=jnp.float32)
        up = jnp.dot(h, wgu_ref[:, D_FF + lo:D_FF + lo + FF_CHUNK], preferred_element_type=jnp.float32)
        a = (gate / (1.0 + jnp.exp(-gate)) * up).astype(jnp.bfloat16)
        acc = acc + jnp.dot(a, wd_ref[lo:lo + FF_CHUNK, :], preferred_element_type=jnp.float32)
    acc = acc + x1
    if final:
        acc = _rms(acc, gfin_ref[...])
    o_ref[...] = acc


def _dense(x2d, ymt, yft, wo_all, g, wgu_all, wd_all, gfin, layer, final):
    t = x2d.shape[0]
    const = lambda shape: pl.BlockSpec(shape, lambda i: (0,) * len(shape),
                                       pipeline_mode=pl.Buffered(1))
    weights = lambda stacked: _layer_block(stacked, layer, pipeline_mode=pl.Buffered(1))
    return pl.pallas_call(
        functools.partial(_dense_kernel, final=final),
        grid=(t // TOKEN_TILE,),
        in_specs=[
            pl.BlockSpec((TOKEN_TILE, D_MODEL), lambda i: (i, 0)),
            pl.BlockSpec((GROUP_WIDTH, TOKEN_TILE), lambda i: (0, i)),
            pl.BlockSpec((GROUP_WIDTH, TOKEN_TILE), lambda i: (0, i)),
            weights(wo_all),
            const((1, D_MODEL)),
            weights(wgu_all),
            weights(wd_all),
            const((1, D_MODEL)),
        ],
        out_specs=pl.BlockSpec((TOKEN_TILE, D_MODEL), lambda i: (i, 0)),
        out_shape=jax.ShapeDtypeStruct((t, D_MODEL), jnp.float32),
        compiler_params=pltpu.CompilerParams(
            dimension_semantics=("arbitrary",), vmem_limit_bytes=VMEM_LIMIT),
        name="dense_ffn",
    )(x2d, ymt, yft, wo_all, g, wgu_all, wd_all, gfin)


def kernel(x, w_in, b_f, w_o, g_attn, w_gu, w_down, g_ffn, rel_bias, g_final):
    b, seq, d = x.shape
    depth = w_in.shape[0]
    assert d == D_MODEL and seq % BLOCK == 0 and (b * seq) % TOKEN_TILE == 0
    assert seq // BLOCK <= BF16_ROWS and C_PARTS * N_HEADS <= HEAD_DIM
    bf16 = jnp.bfloat16
    gw = GROUP_WIDTH
    bias = _bias_strips(rel_bias, seq)
    w_in16, w_o16, w_gu16, w_down16 = (w.astype(bf16) for w in (w_in, w_o, w_gu, w_down))
    w_f = _forget_lane_layout(w_in[:, :, 6 * gw:]).astype(bf16)
    b_lanes = _forget_lane_layout(b_f[:, None, :])
    x2d = x.reshape(b * seq, d)
    for layer in range(depth):
        k2, qvt, logf = _inproj(x2d, g_attn[layer][None, :], w_in16, layer, w_f[layer], b_lanes[layer])
        k3 = k2.reshape(b, seq, 2 * gw)
        ymt = _moba(k3, qvt, bias)
        yft = _fox(k3, qvt, logf.reshape(b, seq, LANES))
        x2d = _dense(x2d, ymt, yft, w_o16, g_ffn[layer][None, :], w_gu16, w_down16,
                     g_final[None, :], layer, final=(layer == depth - 1))
    return x2d.reshape(b, seq, d)
```

```python
import functools
import math

import jax
import jax.numpy as jnp
from jax import lax
from jax.experimental import pallas as pl
from jax.experimental.pallas import tpu as pltpu

D_MODEL = 1024
HEAD_DIM = 64
N_HEADS = 8
GROUP_WIDTH = N_HEADS * HEAD_DIM
BLOCK = 256
TOPK = 3
NUM_BUCKETS = 32
MAX_DISTANCE = 1024
D_FF = 2816
RMS_EPS = 1e-6
SCALE = HEAD_DIM ** -0.5
NEG = -1e30

LANES = 128
SUBLANES = 8
BF16_ROWS = 16
HEADS_PER_PAIR = LANES // HEAD_DIM
PAIRS_PER_STEP = 2
HEADS_IN_STEP = PAIRS_PER_STEP * HEADS_PER_PAIR
STEP_WIDTH = PAIRS_PER_STEP * LANES
N_STEPS = GROUP_WIDTH // STEP_WIDTH
UNITS_PER_GROUP = 2 * HEADS_PER_PAIR
C_PARTS = 3
V_ROWS = HEAD_DIM + BF16_ROWS
SLAB = 16
FF_CHUNK = 256
TOKEN_TILE = 1024
VMEM_LIMIT = 56 * 1024 * 1024

_NT = (((1,), (1,)), ((), ()))
_TN = (((0,), (0,)), ((), ()))


def _rms(x, g):
    return x * lax.rsqrt(jnp.mean(x * x, axis=-1, keepdims=True) + RMS_EPS) * g


def _inproj_kernel(x_ref, g_ref, w_ref, wf_ref, bf_ref, k_ref, qvt_ref, logf_ref):
    gw = GROUP_WIDTH
    h = _rms(x_ref[...], g_ref[...]).astype(jnp.bfloat16)

    def proj(group):
        return jnp.dot(h, w_ref[:, group * gw:(group + 1) * gw], preferred_element_type=jnp.float32)

    for n, group in enumerate((1, 4)):
        k_ref[:, n * gw:(n + 1) * gw] = proj(group).astype(jnp.bfloat16)
    for n, group in enumerate((0, 3, 2, 5)):
        qvt_ref[n * gw:(n + 1) * gw, :] = proj(group).T.astype(jnp.bfloat16)
    z = jnp.dot(h, wf_ref[...], preferred_element_type=jnp.float32) + bf_ref[...]
    logf_ref[...] = -(jnp.maximum(-z, 0.0) + jnp.log1p(jnp.exp(-jnp.abs(z))))


def _layer_block(stacked, layer, **kwargs):
    return pl.BlockSpec((None,) + stacked.shape[1:], lambda i: (layer, 0, 0), **kwargs)


def _inproj(x2d, g, w_all, layer, wf, bf):
    t = x2d.shape[0]
    const = lambda shape: pl.BlockSpec(shape, lambda i: (0, 0))
    return pl.pallas_call(
        _inproj_kernel,
        grid=(t // TOKEN_TILE,),
        in_specs=[
            pl.BlockSpec((TOKEN_TILE, D_MODEL), lambda i: (i, 0)),
            const((1, D_MODEL)),
            _layer_block(w_all, layer),
            const((D_MODEL, LANES)),
            const((1, LANES)),
        ],
        out_specs=[
            pl.BlockSpec((TOKEN_TILE, 2 * GROUP_WIDTH), lambda i: (i, 0)),
            pl.BlockSpec((4 * GROUP_WIDTH, TOKEN_TILE), lambda i: (0, i)),
            pl.BlockSpec((TOKEN_TILE, LANES), lambda i: (i, 0)),
        ],
        out_shape=[
            jax.ShapeDtypeStruct((t, 2 * GROUP_WIDTH), jnp.bfloat16),
            jax.ShapeDtypeStruct((4 * GROUP_WIDTH, t), jnp.bfloat16),
            jax.ShapeDtypeStruct((t, LANES), jnp.float32),
        ],
        compiler_params=pltpu.CompilerParams(
            dimension_semantics=("arbitrary",), vmem_limit_bytes=VMEM_LIMIT),
        name="inproj",
    )(x2d, g, w_all, wf, bf)


def _forget_lane_layout(cols):
    rep = jnp.repeat(cols, C_PARTS, axis=-1)
    half = jnp.pad(rep, [(0, 0)] * (cols.ndim - 1) + [(0, HEAD_DIM - C_PARTS * N_HEADS)])
    return jnp.concatenate([half, half], axis=-1)


def _bias_kernel(relt_ref, bkt_ref, out_ref, *, seq):
    h = pl.program_id(0)
    bkt = bkt_ref[...]
    w = jnp.zeros(bkt.shape, jnp.float32)
    for b in range(NUM_BUCKETS):
        w = jnp.where(bkt == b, relt_ref[h, b], w)
    x = jnp.broadcast_to(w, (BLOCK, bkt.shape[1]))
    y = pltpu.roll(x, 0, 1, stride=1, stride_axis=0)
    out_ref[0] = y[:, BLOCK:].T


def _t5_bucket(dist):
    max_exact = NUM_BUCKETS // 2
    is_small = dist < max_exact
    d = jnp.maximum(dist, 1).astype(jnp.float32)
    large = max_exact + (jnp.log(d / max_exact) / math.log(MAX_DISTANCE / max_exact)
                         * (NUM_BUCKETS - max_exact)).astype(jnp.int32)
    large = jnp.minimum(large, NUM_BUCKETS - 1)
    return jnp.where(is_small, dist, large)


def _bias_strips(rel_bias, seq):
    n = jnp.arange(seq + BLOCK, dtype=jnp.int32)
    bkt = _t5_bucket(jnp.clip(seq - n, 0, seq - 1))[None, :]
    return pl.pallas_call(
        functools.partial(_bias_kernel, seq=seq),
        grid=(N_HEADS,),
        in_specs=[
            pl.BlockSpec(memory_space=pltpu.SMEM),
            pl.BlockSpec((1, seq + BLOCK), lambda h: (0, 0)),
        ],
        out_specs=pl.BlockSpec((1, seq, BLOCK), lambda h: (h, 0, 0)),
        out_shape=jax.ShapeDtypeStruct((N_HEADS, seq, BLOCK), jnp.float32),
        compiler_params=pltpu.CompilerParams(dimension_semantics=("arbitrary",)),
        name="moba_bias",
    )(rel_bias.T, bkt)


def _iota(shape, axis):
    return lax.broadcasted_iota(jnp.int32, shape, axis)


def _split3(x):
    p1 = x.astype(jnp.bfloat16)
    r1 = x - p1.astype(jnp.float32)
    p2 = r1.astype(jnp.bfloat16)
    p3 = (r1 - p2.astype(jnp.float32)).astype(jnp.bfloat16)
    return p1, p2, p3


def _augment_kv(k_ref, vt_ref, k_extra, kp_sc, vt_sc):
    seq = k_ref.shape[1]
    lane = _iota((seq, LANES), 1)
    for h in range(HEADS_IN_STEP):
        pr, hh = divmod(h, HEADS_PER_PAIR)
        hmask = (lane >= hh * HEAD_DIM) & (lane < (hh + 1) * HEAD_DIM)
        kp_sc[h] = jnp.where(hmask, k_ref[0, :, pr * LANES:(pr + 1) * LANES], k_extra)
        vt_sc[h, :HEAD_DIM, :] = vt_ref[h * HEAD_DIM:(h + 1) * HEAD_DIM, :]
        vt_sc[h, HEAD_DIM:, :] = jnp.ones((V_ROWS - HEAD_DIM, seq), vt_sc.dtype)


class _Unit:
    def __init__(self, h, i, qat, tile_term):
        self.h, self.i, self.qat, self.tile_term = h, i, qat, tile_term
        self.s_ref = self.mxp = self.m = self.acc = None


def _score_tile(u, j, kp_sc):
    key_in_tile = _iota((SLAB, BLOCK), 0)
    query_in_tile = _iota((SLAB, BLOCK), 1)
    s = jnp.dot(kp_sc[u.h, j * BLOCK:(j + 1) * BLOCK, :], u.qat, preferred_element_type=jnp.float32)
    for r in range(0, BLOCK, SLAB):
        piece = s[r:r + SLAB, :]
        if u.tile_term is not None:
            piece = piece + u.tile_term(j, r)
        if j == u.i:
            piece = jnp.where(key_in_tile + r <= query_in_tile, piece, NEG)
        u.s_ref[j * BLOCK + r:j * BLOCK + r + SLAB, :] = piece
        t = jnp.max(piece.reshape(SLAB // SUBLANES, SUBLANES, BLOCK), axis=0)
        u.mxp = t if u.mxp is None else jnp.maximum(u.mxp, t)


def _value_tile(u, j, vt_sc):
    if u.m is None:
        u.m = jnp.max(u.mxp, axis=0, keepdims=True)
    e = jnp.concatenate(
        [jnp.exp(u.s_ref[j * BLOCK + r:j * BLOCK + r + SLAB, :] - u.m).astype(jnp.bfloat16)
         for r in range(0, BLOCK, SLAB)], axis=0)
    pv = jnp.dot(vt_sc[u.h, :, j * BLOCK:(j + 1) * BLOCK], e, preferred_element_type=jnp.float32)
    u.acc = pv if u.acc is None else u.acc + pv


def _run_units(nb, make_unit, s_sc, kp_sc, vt_sc, o_ref, unit_major_scores):
    groups = [[(pr * HEADS_PER_PAIR + hh, i) for i in (a, nb - 1 - a) for hh in range(HEADS_PER_PAIR)]
              for pr in range(PAIRS_PER_STEP) for a in range(nb // 2)]
    prev = []
    for g, grp in enumerate(groups + [None]):
        cur = [make_unit(h, i) for h, i in grp] if grp is not None else []
        for n, u in enumerate(cur):
            u.s_ref = s_sc.at[(g % 2) * UNITS_PER_GROUP + n]
        if unit_major_scores:
            s_items = [(u, t) for u in cur for t in range(u.i + 1)]
            v_items = [(u, t) for t in range(nb) for u in prev if t <= u.i]
            for n in range(max(len(s_items), len(v_items))):
                if n < len(s_items):
                    _score_tile(*s_items[n], kp_sc)
                if n < len(v_items):
                    _value_tile(*v_items[n], vt_sc)
        else:
            for t in range(nb):
                for u in cur:
                    if t <= u.i:
                        _score_tile(u, t, kp_sc)
                for u in prev:
                    if t <= u.i:
                        _value_tile(u, t, vt_sc)
        for u in prev:
            out = u.acc[:HEAD_DIM, :] * (1.0 / u.acc[HEAD_DIM:HEAD_DIM + 1, :])
            o_ref[u.h * HEAD_DIM:(u.h + 1) * HEAD_DIM,
                  u.i * BLOCK:(u.i + 1) * BLOCK] = out.astype(o_ref.dtype)
        prev = cur


def _scaled_qt(qt_ref, h, i):
    pr = h // HEADS_PER_PAIR
    q = qt_ref[pr * LANES:(pr + 1) * LANES, i * BLOCK:(i + 1) * BLOCK]
    return q * jnp.asarray(SCALE, q.dtype)


def _with_spare_rows(qt, hh, spare):
    head = qt[hh * HEAD_DIM:(hh + 1) * HEAD_DIM, :]
    spare = jnp.concatenate(
        [spare, jnp.zeros((HEAD_DIM - spare.shape[0], BLOCK), qt.dtype)], axis=0)
    return jnp.concatenate([head, spare] if hh == 0 else [spare, head], axis=0)


def _attn_scratch(seq):
    return [pltpu.VMEM((2 * UNITS_PER_GROUP, seq, BLOCK), jnp.float32),
            pltpu.VMEM((HEADS_IN_STEP, seq, LANES), jnp.bfloat16),
            pltpu.VMEM((HEADS_IN_STEP, V_ROWS, seq), jnp.bfloat16)]


def _moba_kernel(qt_ref, k_ref, vt_ref, bias_ref, o_ref, s_sc, kp_sc, vt_sc, *, seq):
    nb = seq // BLOCK
    key_blk = _iota((seq, LANES), 0) // BLOCK
    onehot = ((_iota((seq, LANES), 1) % HEAD_DIM) == key_blk).astype(k_ref.dtype)
    _augment_kv(k_ref, vt_ref, onehot, kp_sc, vt_sc)
    blk_row = _iota((BF16_ROWS, BLOCK), 0)
    feat_row = _iota((LANES, BLOCK), 0)

    def pair_kmean(pr):
        k = k_ref[0, :, pr * LANES:(pr + 1) * LANES].astype(jnp.float32)
        kmean = jnp.sum(k.reshape(nb, BLOCK, LANES), axis=1) * (1.0 / BLOCK)
        return jnp.concatenate([kmean, jnp.zeros((BF16_ROWS - nb, LANES), jnp.float32)], axis=0)

    kmeans = [pair_kmean(pr) for pr in range(PAIRS_PER_STEP)]

    def make_unit(h, i):
        pr, hh = divmod(h, HEADS_PER_PAIR)
        qt = _scaled_qt(qt_ref, h, i)
        pen = jnp.zeros((BF16_ROWS, BLOCK), qt.dtype)
        if i > TOPK:
            in_head = (feat_row >= hh * HEAD_DIM) & (feat_row < (hh + 1) * HEAD_DIM)
            q_head = jnp.where(in_head, qt, jnp.zeros_like(qt)).astype(jnp.float32)
            gate = jnp.dot(kmeans[pr], q_head, precision=lax.Precision.HIGHEST,
                           preferred_element_type=jnp.float32)
            rank = jnp.zeros((BF16_ROWS, BLOCK), jnp.int32)
            for jp in range(i):
                other = gate[jp:jp + 1, :]
                beats = (other > gate) | ((other == gate) & (blk_row > jp))
                rank = rank + beats.astype(jnp.int32)
            pen = jnp.where((blk_row < i) & (rank >= TOPK), NEG, 0.0).astype(qt.dtype)

        def tile_term(j, r):
            off = (nb - 1 - i + j) * BLOCK + r
            return bias_ref[h, off:off + SLAB, :]

        return _Unit(h, i, _with_spare_rows(qt, hh, pen), tile_term)

    _run_units(nb, make_unit, s_sc, kp_sc, vt_sc, o_ref, unit_major_scores=False)


def _moba(k3, qvt, bias):
    b, seq, _ = k3.shape
    return pl.pallas_call(
        functools.partial(_moba_kernel, seq=seq),
        grid=(N_STEPS, b),
        in_specs=[
            pl.BlockSpec((STEP_WIDTH, seq), lambda p, bi: (p, bi)),
            pl.BlockSpec((1, seq, STEP_WIDTH), lambda p, bi: (bi, 0, p)),
            pl.BlockSpec((STEP_WIDTH, seq), lambda p, bi: (2 * N_STEPS + p, bi)),
            pl.BlockSpec((HEADS_IN_STEP, seq, BLOCK), lambda p, bi: (p, 0, 0)),
        ],
        out_specs=pl.BlockSpec((STEP_WIDTH, seq), lambda p, bi: (p, bi)),
        out_shape=jax.ShapeDtypeStruct((GROUP_WIDTH, b * seq), jnp.bfloat16),
        scratch_shapes=_attn_scratch(seq),
        compiler_params=pltpu.CompilerParams(
            dimension_semantics=("arbitrary", "arbitrary"), vmem_limit_bytes=VMEM_LIMIT),
        name="moba_attn",
    )(qvt, k3, qvt, bias)


def _fox_kernel(qt_ref, k_ref, vt_ref, logf_ref, o_ref, ext_sc, s_sc, kp_sc, vt_sc, *, seq):
    nb = seq // BLOCK
    p = pl.program_id(1)

    @pl.when(p == 0)
    def _():
        tri = (_iota((BLOCK, BLOCK), 1) <= _iota((BLOCK, BLOCK), 0)).astype(jnp.bfloat16)
        lane = _iota((BLOCK, LANES), 1) % HEAD_DIM
        carry = jnp.zeros((1, LANES), jnp.float32)
        for ch in range(nb):
            parts = _split3(logf_ref[ch * BLOCK:(ch + 1) * BLOCK, :])
            cs = jnp.dot(tri, jnp.concatenate(parts, axis=1), preferred_element_type=jnp.float32)
            c = cs[:, :LANES] + cs[:, LANES:2 * LANES] + cs[:, 2 * LANES:] + carry
            carry = c[BLOCK - 1:BLOCK, :]
            p1, p2, p3 = _split3(-c)
            part = lane % C_PARTS
            ext = jnp.where(part == 0, p1, jnp.where(part == 1, p2, p3))
            ext_sc[ch * BLOCK:(ch + 1) * BLOCK, :] = jnp.where(
                lane < C_PARTS * N_HEADS, ext, jnp.zeros_like(ext))

    _augment_kv(k_ref, vt_ref, ext_sc[...], kp_sc, vt_sc)
    n_spare = -(-C_PARTS * N_HEADS // BF16_ROWS) * BF16_ROWS
    row = _iota((n_spare, BLOCK), 0)

    def make_unit(h, i):
        qt = _scaled_qt(qt_ref, h, i)
        first = C_PARTS * (HEADS_IN_STEP * p + h)
        spare = jnp.where((row >= first) & (row < first + C_PARTS), 1.0, 0.0).astype(qt.dtype)
        return _Unit(h, i, _with_spare_rows(qt, h % HEADS_PER_PAIR, spare), None)

    _run_units(nb, make_unit, s_sc, kp_sc, vt_sc, o_ref, unit_major_scores=True)


def _fox(k3, qvt, logf3):
    b, seq, _ = k3.shape
    return pl.pallas_call(
        functools.partial(_fox_kernel, seq=seq),
        grid=(b, N_STEPS),
        in_specs=[
            pl.BlockSpec((STEP_WIDTH, seq), lambda bi, p: (N_STEPS + p, bi)),
            pl.BlockSpec((1, seq, STEP_WIDTH), lambda bi, p: (bi, 0, N_STEPS + p)),
            pl.BlockSpec((STEP_WIDTH, seq), lambda bi, p: (3 * N_STEPS + p, bi)),
            pl.BlockSpec((None, seq, LANES), lambda bi, p: (bi, 0, 0)),
        ],
        out_specs=pl.BlockSpec((STEP_WIDTH, seq), lambda bi, p: (p, bi)),
        out_shape=jax.ShapeDtypeStruct((GROUP_WIDTH, b * seq), jnp.bfloat16),
        scratch_shapes=[pltpu.VMEM((seq, LANES), jnp.bfloat16)] + _attn_scratch(seq),
        compiler_params=pltpu.CompilerParams(
            dimension_semantics=("arbitrary", "arbitrary"), vmem_limit_bytes=VMEM_LIMIT),
        name="fox_attn",
    )(qvt, k3, qvt, logf3)


def _dense_kernel(x_ref, ymt_ref, yft_ref, wo_ref, g_ref, wgu_ref, wd_ref, gfin_ref, o_ref, *, final):
    x1 = (x_ref[...]
          + lax.dot_general(ymt_ref[...], wo_ref[:GROUP_WIDTH, :], _TN,
                            preferred_element_type=jnp.float32)
          + lax.dot_general(yft_ref[...], wo_ref[GROUP_WIDTH:, :], _TN,
                            preferred_element_type=jnp.float32))
    h = _rms(x1, g_ref[...]).astype(jnp.bfloat16)
    acc = jnp.zeros_like(x1)
    for c in range(D_FF // FF_CHUNK):
        lo = c * FF_CHUNK
        gate = jnp.dot(h, wgu_ref[:, lo:lo + FF_CHUNK], preferred_element_type=jnp.float32)
        up = jnp.dot(h, wgu_ref[:, D_FF + lo:D_FF + lo + FF_CHUNK], preferred_element_type=jnp.float32)
        a = (gate / (1.0 + jnp.exp(-gate)) * up).astype(jnp.bfloat16)
        acc = acc + jnp.dot(a, wd_ref[lo:lo + FF_CHUNK, :], preferred_element_type=jnp.float32)
    acc = acc + x1
    if final:
        acc = _rms(acc, gfin_ref[...])
    o_ref[...] = acc


def _dense(x2d, ymt, yft, wo_all, g, wgu_all, wd_all, gfin, layer, final):
    t = x2d.shape[0]
    const = lambda shape: pl.BlockSpec(shape, lambda i: (0,) * len(shape),
                                       pipeline_mode=pl.Buffered(1))
    weights = lambda stacked: _layer_block(stacked, layer, pipeline_mode=pl.Buffered(1))
    return pl.pallas_call(
        functools.partial(_dense_kernel, final=final),
        grid=(t // TOKEN_TILE,),
        in_specs=[
            pl.BlockSpec((TOKEN_TILE, D_MODEL), lambda i: (i, 0)),
            pl.BlockSpec((GROUP_WIDTH, TOKEN_TILE), lambda i: (0, i)),
            pl.BlockSpec((GROUP_WIDTH, TOKEN_TILE), lambda i: (0, i)),
            weights(wo_all),
            const((1, D_MODEL)),
            weights(wgu_all),
            weights(wd_all),
            const((1, D_MODEL)),
        ],
        out_specs=pl.BlockSpec((TOKEN_TILE, D_MODEL), lambda i: (i, 0)),
        out_shape=jax.ShapeDtypeStruct((t, D_MODEL), jnp.float32),
        compiler_params=pltpu.CompilerParams(
            dimension_semantics=("arbitrary",), vmem_limit_bytes=VMEM_LIMIT),
        name="dense_ffn",
    )(x2d, ymt, yft, wo_all, g, wgu_all, wd_all, gfin)


def kernel(x, w_in, b_f, w_o, g_attn, w_gu, w_down, g_ffn, rel_bias, g_final):
    b, seq, d = x.shape
    depth = w_in.shape[0]
    assert d == D_MODEL and seq % BLOCK == 0 and (b * seq) % TOKEN_TILE == 0
    assert seq // BLOCK <= BF16_ROWS and C_PARTS * N_HEADS <= HEAD_DIM
    bf16 = jnp.bfloat16
    gw = GROUP_WIDTH
    bias = _bias_strips(rel_bias, seq)
    w_in16, w_o16, w_gu16, w_down16 = (w.astype(bf16) for w in (w_in, w_o, w_gu, w_down))
    w_f = _forget_lane_layout(w_in[:, :, 6 * gw:]).astype(bf16)
    b_lanes = _forget_lane_layout(b_f[:, None, :])
    x2d = x.reshape(b * seq, d)
    for layer in range(depth):
        k2, qvt, logf = _inproj(x2d, g_attn[layer][None, :], w_in16, layer, w_f[layer], b_lanes[layer])
        k3 = k2.reshape(b, seq, 2 * gw)
        ymt = _moba(k3, qvt, bias)
        yft = _fox(k3, qvt, logf.reshape(b, seq, LANES))
        x2d = _dense(x2d, ymt, yft, w_o16, g_ffn[layer][None, :], w_gu16, w_down16,
                     g_final[None, :], layer, final=(layer == depth - 1))
    return x2d.reshape(b, seq, d)
```

```python
import functools
import math

import jax
import jax.numpy as jnp
from jax import lax
from jax.experimental import pallas as pl
from jax.experimental.pallas import tpu as pltpu

D_MODEL = 1024
HEAD_DIM = 64
N_HEADS = 8
GROUP_WIDTH = N_HEADS * HEAD_DIM
BLOCK = 256
TOPK = 3
NUM_BUCKETS = 32
MAX_DISTANCE = 1024
D_FF = 2816
RMS_EPS = 1e-6
SCALE = HEAD_DIM ** -0.5
NEG = -1e30

LANES = 128
SUBLANES = 8
BF16_ROWS = 16
HEADS_PER_PAIR = LANES // HEAD_DIM
PAIRS_PER_STEP = 2
HEADS_IN_STEP = PAIRS_PER_STEP * HEADS_PER_PAIR
STEP_WIDTH = PAIRS_PER_STEP * LANES
N_STEPS = GROUP_WIDTH // STEP_WIDTH
UNITS_PER_GROUP = 2 * HEADS_PER_PAIR
C_PARTS = 3
V_ROWS = HEAD_DIM + BF16_ROWS
SLAB = 16
FF_CHUNK = 256
TOKEN_TILE = 1024
VMEM_LIMIT = 56 * 1024 * 1024

_NT = (((1,), (1,)), ((), ()))
_TN = (((0,), (0,)), ((), ()))


def _rms(x, g):
    return x * lax.rsqrt(jnp.mean(x * x, axis=-1, keepdims=True) + RMS_EPS) * g


def _inproj_kernel(x_ref, g_ref, w_ref, wf_ref, bf_ref, k_ref, qvt_ref, logf_ref):
    gw = GROUP_WIDTH
    h = _rms(x_ref[...], g_ref[...]).astype(jnp.bfloat16)

    def proj(group):
        return jnp.dot(h, w_ref[:, group * gw:(group + 1) * gw], preferred_element_type=jnp.float32)

    for n, group in enumerate((1, 4)):
        k_ref[:, n * gw:(n + 1) * gw] = proj(group).astype(jnp.bfloat16)
    for n, group in enumerate((0, 3, 2, 5)):
        qvt_ref[n * gw:(n + 1) * gw, :] = proj(group).T.astype(jnp.bfloat16)
    z = jnp.dot(h, wf_ref[...], preferred_element_type=jnp.float32) + bf_ref[...]
    logf_ref[...] = -(jnp.maximum(-z, 0.0) + jnp.log1p(jnp.exp(-jnp.abs(z))))


def _layer_block(stacked, layer, **kwargs):
    return pl.BlockSpec((None,) + stacked.shape[1:], lambda i: (layer, 0, 0), **kwargs)


def _inproj(x2d, g, w_all, layer, wf, bf):
    t = x2d.shape[0]
    const = lambda shape: pl.BlockSpec(shape, lambda i: (0, 0))
    return pl.pallas_call(
        _inproj_kernel,
        grid=(t // TOKEN_TILE,),
        in_specs=[
            pl.BlockSpec((TOKEN_TILE, D_MODEL), lambda i: (i, 0)),
            const((1, D_MODEL)),
            _layer_block(w_all, layer),
            const((D_MODEL, LANES)),
            const((1, LANES)),
        ],
        out_specs=[
            pl.BlockSpec((TOKEN_TILE, 2 * GROUP_WIDTH), lambda i: (i, 0)),
            pl.BlockSpec((4 * GROUP_WIDTH, TOKEN_TILE), lambda i: (0, i)),
            pl.BlockSpec((TOKEN_TILE, LANES), lambda i: (i, 0)),
        ],
        out_shape=[
            jax.ShapeDtypeStruct((t, 2 * GROUP_WIDTH), jnp.bfloat16),
            jax.ShapeDtypeStruct((4 * GROUP_WIDTH, t), jnp.bfloat16),
            jax.ShapeDtypeStruct((t, LANES), jnp.float32),
        ],
        compiler_params=pltpu.CompilerParams(
            dimension_semantics=("arbitrary",), vmem_limit_bytes=VMEM_LIMIT),
        name="inproj",
    )(x2d, g, w_all, wf, bf)


def _forget_lane_layout(cols):
    rep = jnp.repeat(cols, C_PARTS, axis=-1)
    half = jnp.pad(rep, [(0, 0)] * (cols.ndim - 1) + [(0, HEAD_DIM - C_PARTS * N_HEADS)])
    return jnp.concatenate([half, half], axis=-1)


def _bias_kernel(relt_ref, bkt_ref, out_ref, *, seq):
    h = pl.program_id(0)
    bkt = bkt_ref[...]
    w = jnp.zeros(bkt.shape, jnp.float32)
    for b in range(NUM_BUCKETS):
        w = jnp.where(bkt == b, relt_ref[h, b], w)
    x = jnp.broadcast_to(w, (BLOCK, bkt.shape[1]))
    y = pltpu.roll(x, 0, 1, stride=1, stride_axis=0)
    out_ref[0] = y[:, BLOCK:].T


def _t5_bucket(dist):
    max_exact = NUM_BUCKETS // 2
    is_small = dist < max_exact
    d = jnp.maximum(dist, 1).astype(jnp.float32)
    large = max_exact + (jnp.log(d / max_exact) / math.log(MAX_DISTANCE / max_exact)
                         * (NUM_BUCKETS - max_exact)).astype(jnp.int32)
    large = jnp.minimum(large, NUM_BUCKETS - 1)
    return jnp.where(is_small, dist, large)


def _bias_strips(rel_bias, seq):
    n = jnp.arange(seq + BLOCK, dtype=jnp.int32)
    bkt = _t5_bucket(jnp.clip(seq - n, 0, seq - 1))[None, :]
    return pl.pallas_call(
        functools.partial(_bias_kernel, seq=seq),
        grid=(N_HEADS,),
        in_specs=[
            pl.BlockSpec(memory_space=pltpu.SMEM),
            pl.BlockSpec((1, seq + BLOCK), lambda h: (0, 0)),
        ],
        out_specs=pl.BlockSpec((1, seq, BLOCK), lambda h: (h, 0, 0)),
        out_shape=jax.ShapeDtypeStruct((N_HEADS, seq, BLOCK), jnp.float32),
        compiler_params=pltpu.CompilerParams(dimension_semantics=("arbitrary",)),
        name="moba_bias",
    )(rel_bias.T, bkt)


def _iota(shape, axis):
    return lax.broadcasted_iota(jnp.int32, shape, axis)


def _split3(x):
    p1 = x.astype(jnp.bfloat16)
    r1 = x - p1.astype(jnp.float32)
    p2 = r1.astype(jnp.bfloat16)
    p3 = (r1 - p2.astype(jnp.float32)).astype(jnp.bfloat16)
    return p1, p2, p3


def _augment_kv(k_ref, vt_ref, k_extra, kp_sc, vt_sc):
    seq = k_ref.shape[1]
    lane = _iota((seq, LANES), 1)
    for h in range(HEADS_IN_STEP):
        pr, hh = divmod(h, HEADS_PER_PAIR)
        hmask = (lane >= hh * HEAD_DIM) & (lane < (hh + 1) * HEAD_DIM)
        kp_sc[h] = jnp.where(hmask, k_ref[0, :, pr * LANES:(pr + 1) * LANES], k_extra)
        vt_sc[h, :HEAD_DIM, :] = vt_ref[h * HEAD_DIM:(h + 1) * HEAD_DIM, :]
        vt_sc[h, HEAD_DIM:, :] = jnp.ones((V_ROWS - HEAD_DIM, seq), vt_sc.dtype)


class _Unit:
    def __init__(self, h, i, qat, tile_term):
        self.h, self.i, self.qat, self.tile_term = h, i, qat, tile_term
        self.s_ref = self.mxp = self.m = self.acc = None


def _score_tile(u, j, kp_sc):
    key_in_tile = _iota((SLAB, BLOCK), 0)
    query_in_tile = _iota((SLAB, BLOCK), 1)
    s = jnp.dot(kp_sc[u.h, j * BLOCK:(j + 1) * BLOCK, :], u.qat, preferred_element_type=jnp.float32)
    for r in range(0, BLOCK, SLAB):
        piece = s[r:r + SLAB, :]
        if u.tile_term is not None:
            piece = piece + u.tile_term(j, r)
        if j == u.i:
            piece = jnp.where(key_in_tile + r <= query_in_tile, piece, NEG)
        u.s_ref[j * BLOCK + r:j * BLOCK + r + SLAB, :] = piece
        t = jnp.max(piece.reshape(SLAB // SUBLANES, SUBLANES, BLOCK), axis=0)
        u.mxp = t if u.mxp is None else jnp.maximum(u.mxp, t)


def _value_tile(u, j, vt_sc):
    if u.m is None:
        u.m = jnp.max(u.mxp, axis=0, keepdims=True)
    e = jnp.concatenate(
        [jnp.exp(u.s_ref[j * BLOCK + r:j * BLOCK + r + SLAB, :] - u.m).astype(jnp.bfloat16)
         for r in range(0, BLOCK, SLAB)], axis=0)
    pv = jnp.dot(vt_sc[u.h, :, j * BLOCK:(j + 1) * BLOCK], e, preferred_element_type=jnp.float32)
    u.acc = pv if u.acc is None else u.acc + pv


def _run_units(nb, make_unit, s_sc, kp_sc, vt_sc, o_ref, unit_major_scores):
    groups = [[(pr * HEADS_PER_PAIR + hh, i) for i in (a, nb - 1 - a) for hh in range(HEADS_PER_PAIR)]
              for pr in range(PAIRS_PER_STEP) for a in range(nb // 2)]
    prev = []
    for g, grp in enumerate(groups + [None]):
        cur = [make_unit(h, i) for h, i in grp] if grp is not None else []
        for n, u in enumerate(cur):
            u.s_ref = s_sc.at[(g % 2) * UNITS_PER_GROUP + n]
        if unit_major_scores:
            s_items = [(u, t) for u in cur for t in range(u.i + 1)]
        else:
            s_items = [(u, t) for t in range(nb) for u in cur if t <= u.i]
        v_items = [(u, t) for t in range(nb) for u in prev if t <= u.i]
        for n in range(max(len(s_items), len(v_items))):
            if n < len(s_items):
                _score_tile(*s_items[n], kp_sc)
            if n < len(v_items):
                _value_tile(*v_items[n], vt_sc)
        for u in prev:
            out = u.acc[:HEAD_DIM, :] * (1.0 / u.acc[HEAD_DIM:HEAD_DIM + 1, :])
            o_ref[u.h * HEAD_DIM:(u.h + 1) * HEAD_DIM,
                  u.i * BLOCK:(u.i + 1) * BLOCK] = out.astype(o_ref.dtype)
        prev = cur


def _scaled_qt(qt_ref, h, i):
    pr = h // HEADS_PER_PAIR
    q = qt_ref[pr * LANES:(pr + 1) * LANES, i * BLOCK:(i + 1) * BLOCK]
    return q * jnp.asarray(SCALE, q.dtype)


def _with_spare_rows(qt, hh, spare):
    head = qt[hh * HEAD_DIM:(hh + 1) * HEAD_DIM, :]
    spare = jnp.concatenate(
        [spare, jnp.zeros((HEAD_DIM - spare.shape[0], BLOCK), qt.dtype)], axis=0)
    return jnp.concatenate([head, spare] if hh == 0 else [spare, head], axis=0)


def _attn_scratch(seq):
    return [pltpu.VMEM((2 * UNITS_PER_GROUP, seq, BLOCK), jnp.float32),
            pltpu.VMEM((HEADS_IN_STEP, seq, LANES), jnp.bfloat16),
            pltpu.VMEM((HEADS_IN_STEP, V_ROWS, seq), jnp.bfloat16)]


def _moba_kernel(qt_ref, k_ref, vt_ref, bias_ref, o_ref, s_sc, kp_sc, vt_sc, *, seq):
    nb = seq // BLOCK
    key_blk = _iota((seq, LANES), 0) // BLOCK
    onehot = ((_iota((seq, LANES), 1) % HEAD_DIM) == key_blk).astype(k_ref.dtype)
    _augment_kv(k_ref, vt_ref, onehot, kp_sc, vt_sc)
    blk_row = _iota((BF16_ROWS, BLOCK), 0)
    feat_row = _iota((LANES, BLOCK), 0)

    def pair_kmean(pr):
        k = k_ref[0, :, pr * LANES:(pr + 1) * LANES].astype(jnp.float32)
        kmean = jnp.sum(k.reshape(nb, BLOCK, LANES), axis=1) * (1.0 / BLOCK)
        return jnp.concatenate([kmean, jnp.zeros((BF16_ROWS - nb, LANES), jnp.float32)], axis=0)

    kmeans = [pair_kmean(pr) for pr in range(PAIRS_PER_STEP)]

    def make_unit(h, i):
        pr, hh = divmod(h, HEADS_PER_PAIR)
        qt = _scaled_qt(qt_ref, h, i)
        pen = jnp.zeros((BF16_ROWS, BLOCK), qt.dtype)
        if i > TOPK:
            in_head = (feat_row >= hh * HEAD_DIM) & (feat_row < (hh + 1) * HEAD_DIM)
            q_head = jnp.where(in_head, qt, jnp.zeros_like(qt)).astype(jnp.float32)
            gate = jnp.dot(kmeans[pr], q_head, precision=lax.Precision.HIGHEST,
                           preferred_element_type=jnp.float32)
            rank = jnp.zeros((BF16_ROWS, BLOCK), jnp.int32)
            for jp in range(i):
                other = gate[jp:jp + 1, :]
                beats = (other > gate) | ((other == gate) & (blk_row > jp))
                rank = rank + beats.astype(jnp.int32)
            pen = jnp.where((blk_row < i) & (rank >= TOPK), NEG, 0.0).astype(qt.dtype)

        def tile_term(j, r):
            off = (nb - 1 - i + j) * BLOCK + r
            return bias_ref[h, off:off + SLAB, :]

        return _Unit(h, i, _with_spare_rows(qt, hh, pen), tile_term)

    _run_units(nb, make_unit, s_sc, kp_sc, vt_sc, o_ref, unit_major_scores=False)


def _moba(k3, qvt, bias):
    b, seq, _ = k3.shape
    return pl.pallas_call(
        functools.partial(_moba_kernel, seq=seq),
        grid=(N_STEPS, b),
        in_specs=[
            pl.BlockSpec((STEP_WIDTH, seq), lambda p, bi: (p, bi)),
            pl.BlockSpec((1, seq, STEP_WIDTH), lambda p, bi: (bi, 0, p)),
            pl.BlockSpec((STEP_WIDTH, seq), lambda p, bi: (2 * N_STEPS + p, bi)),
            pl.BlockSpec((HEADS_IN_STEP, seq, BLOCK), lambda p, bi: (p, 0, 0)),
        ],
        out_specs=pl.BlockSpec((STEP_WIDTH, seq), lambda p, bi: (p, bi)),
        out_shape=jax.ShapeDtypeStruct((GROUP_WIDTH, b * seq), jnp.bfloat16),
        scratch_shapes=_attn_scratch(seq),
        compiler_params=pltpu.CompilerParams(
            dimension_semantics=("arbitrary", "arbitrary"), vmem_limit_bytes=VMEM_LIMIT),
        name="moba_attn",
    )(qvt, k3, qvt, bias)


def _fox_kernel(qt_ref, k_ref, vt_ref, logf_ref, o_ref, ext_sc, s_sc, kp_sc, vt_sc, *, seq):
    nb = seq // BLOCK
    p = pl.program_id(1)

    @pl.when(p == 0)
    def _():
        tri = (_iota((BLOCK, BLOCK), 1) <= _iota((BLOCK, BLOCK), 0)).astype(jnp.bfloat16)
        lane = _iota((BLOCK, LANES), 1) % HEAD_DIM
        carry = jnp.zeros((1, LANES), jnp.float32)
        for ch in range(nb):
            parts = _split3(logf_ref[ch * BLOCK:(ch + 1) * BLOCK, :])
            cs = jnp.dot(tri, jnp.concatenate(parts, axis=1), preferred_element_type=jnp.float32)
            c = cs[:, :LANES] + cs[:, LANES:2 * LANES] + cs[:, 2 * LANES:] + carry
            carry = c[BLOCK - 1:BLOCK, :]
            p1, p2, p3 = _split3(-c)
            part = lane % C_PARTS
            ext = jnp.where(part == 0, p1, jnp.where(part == 1, p2, p3))
            ext_sc[ch * BLOCK:(ch + 1) * BLOCK, :] = jnp.where(
                lane < C_PARTS * N_HEADS, ext, jnp.zeros_like(ext))

    _augment_kv(k_ref, vt_ref, ext_sc[...], kp_sc, vt_sc)
    n_spare = -(-C_PARTS * N_HEADS // BF16_ROWS) * BF16_ROWS
    row = _iota((n_spare, BLOCK), 0)

    def make_unit(h, i):
        qt = _scaled_qt(qt_ref, h, i)
        first = C_PARTS * (HEADS_IN_STEP * p + h)
        spare = jnp.where((row >= first) & (row < first + C_PARTS), 1.0, 0.0).astype(qt.dtype)
        return _Unit(h, i, _with_spare_rows(qt, h % HEADS_PER_PAIR, spare), None)

    _run_units(nb, make_unit, s_sc, kp_sc, vt_sc, o_ref, unit_major_scores=True)


def _fox(k3, qvt, logf3):
    b, seq, _ = k3.shape
    return pl.pallas_call(
        functools.partial(_fox_kernel, seq=seq),
        grid=(b, N_STEPS),
        in_specs=[
            pl.BlockSpec((STEP_WIDTH, seq), lambda bi, p: (N_STEPS + p, bi)),
            pl.BlockSpec((1, seq, STEP_WIDTH), lambda bi, p: (bi, 0, N_STEPS + p)),
            pl.BlockSpec((STEP_WIDTH, seq), lambda bi, p: (3 * N_STEPS + p, bi)),
            pl.BlockSpec((None, seq, LANES), lambda bi, p: (bi, 0, 0)),
        ],
        out_specs=pl.BlockSpec((STEP_WIDTH, seq), lambda bi, p: (p, bi)),
        out_shape=jax.ShapeDtypeStruct((GROUP_WIDTH, b * seq), jnp.bfloat16),
        scratch_shapes=[pltpu.VMEM((seq, LANES), jnp.bfloat16)] + _attn_scratch(seq),
        compiler_params=pltpu.CompilerParams(
            dimension_semantics=("arbitrary", "arbitrary"), vmem_limit_bytes=VMEM_LIMIT),
        name="fox_attn",
    )(qvt, k3, qvt, logf3)


def _dense_kernel(x_ref, ymt_ref, yft_ref, wo_ref, g_ref, wgu_ref, wd_ref, gfin_ref, o_ref, *, final):
    x1 = (x_ref[...]
          + lax.dot_general(ymt_ref[...], wo_ref[:GROUP_WIDTH, :], _TN,
                            preferred_element_type=jnp.float32)
          + lax.dot_general(yft_ref[...], wo_ref[GROUP_WIDTH:, :], _TN,
                            preferred_element_type=jnp.float32))
    h = _rms(x1, g_ref[...]).astype(jnp.bfloat16)
    acc = jnp.zeros_like(x1)
    for c in range(D_FF // FF_CHUNK):
        lo = c * FF_CHUNK
        gate = jnp.dot(h, wgu_ref[:, lo:lo + FF_CHUNK], preferred_element_type=jnp.float32)
        up = jnp.dot(h, wgu_ref[:, D_FF + lo:D_FF + lo + FF_CHUNK], preferred_element_type=jnp.float32)
        a = (gate / (1.0 + jnp.exp(-gate)) * up).astype(jnp.bfloat16)
        acc = acc + jnp.dot(a, wd_ref[lo:lo + FF_CHUNK, :], preferred_element_type=jnp.float32)
    acc = acc + x1
    if final:
        acc = _rms(acc, gfin_ref[...])
    o_ref[...] = acc


def _dense(x2d, ymt, yft, wo_all, g, wgu_all, wd_all, gfin, layer, final):
    t = x2d.shape[0]
    const = lambda shape: pl.BlockSpec(shape, lambda i: (0,) * len(shape),
                                       pipeline_mode=pl.Buffered(1))
    weights = lambda stacked: _layer_block(stacked, layer, pipeline_mode=pl.Buffered(1))
    return pl.pallas_call(
        functools.partial(_dense_kernel, final=final),
        grid=(t // TOKEN_TILE,),
        in_specs=[
            pl.BlockSpec((TOKEN_TILE, D_MODEL), lambda i: (i, 0)),
            pl.BlockSpec((GROUP_WIDTH, TOKEN_TILE), lambda i: (0, i)),
            pl.BlockSpec((GROUP_WIDTH, TOKEN_TILE), lambda i: (0, i)),
            weights(wo_all),
            const((1, D_MODEL)),
            weights(wgu_all),
            weights(wd_all),
            const((1, D_MODEL)),
        ],
        out_specs=pl.BlockSpec((TOKEN_TILE, D_MODEL), lambda i: (i, 0)),
        out_shape=jax.ShapeDtypeStruct((t, D_MODEL), jnp.float32),
        compiler_params=pltpu.CompilerParams(
            dimension_semantics=("arbitrary",), vmem_limit_bytes=VMEM_LIMIT),
        name="dense_ffn",
    )(x2d, ymt, yft, wo_all, g, wgu_all, wd_all, gfin)


def kernel(x, w_in, b_f, w_o, g_attn, w_gu, w_down, g_ffn, rel_bias, g_final):
    b, seq, d = x.shape
    depth = w_in.shape[0]
    assert d == D_MODEL and seq % BLOCK == 0 and (b * seq) % TOKEN_TILE == 0
    assert seq // BLOCK <= BF16_ROWS and C_PARTS * N_HEADS <= HEAD_DIM
    bf16 = jnp.bfloat16
    gw = GROUP_WIDTH
    bias = _bias_strips(rel_bias, seq)
    w_in16, w_o16, w_gu16, w_down16 = (w.astype(bf16) for w in (w_in, w_o, w_gu, w_down))
    w_f = _forget_lane_layout(w_in[:, :, 6 * gw:]).astype(bf16)
    b_lanes = _forget_lane_layout(b_f[:, None, :])
    x2d = x.reshape(b * seq, d)
    for layer in range(depth):
        k2, qvt, logf = _inproj(x2d, g_attn[layer][None, :], w_in16, layer, w_f[layer], b_lanes[layer])
        k3 = k2.reshape(b, seq, 2 * gw)
        ymt = _moba(k3, qvt, bias)
        yft = _fox(k3, qvt, logf.reshape(b, seq, LANES))
        x2d = _dense(x2d, ymt, yft, w_o16, g_ffn[layer][None, :], w_gu16, w_down16,
                     g_final[None, :], layer, final=(layer == depth - 1))
    return x2d.reshape(b, seq, d)
```

```python
import functools
import math

import jax
import jax.numpy as jnp
from jax import lax
from jax.experimental import pallas as pl
from jax.experimental.pallas import tpu as pltpu

D_MODEL = 1024
HEAD_DIM = 64
N_HEADS = 8
GROUP_WIDTH = N_HEADS * HEAD_DIM
BLOCK = 256
TOPK = 3
NUM_BUCKETS = 32
MAX_DISTANCE = 1024
D_FF = 2816
RMS_EPS = 1e-6
SCALE = HEAD_DIM ** -0.5
NEG = -1e30

LANES = 128
SUBLANES = 8
BF16_ROWS = 16
HEADS_PER_PAIR = LANES // HEAD_DIM
PAIRS_PER_STEP = 2
HEADS_IN_STEP = PAIRS_PER_STEP * HEADS_PER_PAIR
STEP_WIDTH = PAIRS_PER_STEP * LANES
N_STEPS = GROUP_WIDTH // STEP_WIDTH
UNITS_PER_GROUP = 2 * HEADS_PER_PAIR
C_PARTS = 3
V_ROWS = HEAD_DIM + BF16_ROWS
SLAB = 16
FF_CHUNK = 256
TOKEN_TILE = 1024
VMEM_LIMIT = 56 * 1024 * 1024

_NT = (((1,), (1,)), ((), ()))
_TN = (((0,), (0,)), ((), ()))


def _rms(x, g):
    return x * lax.rsqrt(jnp.mean(x * x, axis=-1, keepdims=True) + RMS_EPS) * g


def _inproj_kernel(x_ref, g_ref, w_ref, wf_ref, bf_ref, k_ref, qvt_ref, logf_ref):
    gw = GROUP_WIDTH
    h = _rms(x_ref[...], g_ref[...]).astype(jnp.bfloat16)

    def proj(group):
        return jnp.dot(h, w_ref[:, group * gw:(group + 1) * gw], preferred_element_type=jnp.float32)

    for n, group in enumerate((1, 4)):
        k_ref[:, n * gw:(n + 1) * gw] = proj(group).astype(jnp.bfloat16)
    for n, group in enumerate((0, 3, 2, 5)):
        qvt_ref[n * gw:(n + 1) * gw, :] = proj(group).T.astype(jnp.bfloat16)
    z = jnp.dot(h, wf_ref[...], preferred_element_type=jnp.float32) + bf_ref[...]
    logf_ref[...] = -(jnp.maximum(-z, 0.0) + jnp.log1p(jnp.exp(-jnp.abs(z))))


def _layer_block(stacked, layer, **kwargs):
    return pl.BlockSpec((None,) + stacked.shape[1:], lambda i: (layer, 0, 0), **kwargs)


def _inproj(x2d, g, w_all, layer, wf, bf):
    t = x2d.shape[0]
    const = lambda shape: pl.BlockSpec(shape, lambda i: (0, 0))
    return pl.pallas_call(
        _inproj_kernel,
        grid=(t // TOKEN_TILE,),
        in_specs=[
            pl.BlockSpec((TOKEN_TILE, D_MODEL), lambda i: (i, 0)),
            const((1, D_MODEL)),
            _layer_block(w_all, layer),
            const((D_MODEL, LANES)),
            const((1, LANES)),
        ],
        out_specs=[
            pl.BlockSpec((TOKEN_TILE, 2 * GROUP_WIDTH), lambda i: (i, 0)),
            pl.BlockSpec((4 * GROUP_WIDTH, TOKEN_TILE), lambda i: (0, i)),
            pl.BlockSpec((TOKEN_TILE, LANES), lambda i: (i, 0)),
        ],
        out_shape=[
            jax.ShapeDtypeStruct((t, 2 * GROUP_WIDTH), jnp.bfloat16),
            jax.ShapeDtypeStruct((4 * GROUP_WIDTH, t), jnp.bfloat16),
            jax.ShapeDtypeStruct((t, LANES), jnp.float32),
        ],
        compiler_params=pltpu.CompilerParams(
            dimension_semantics=("arbitrary",), vmem_limit_bytes=VMEM_LIMIT),
        name="inproj",
    )(x2d, g, w_all, wf, bf)


def _forget_lane_layout(cols):
    rep = jnp.repeat(cols, C_PARTS, axis=-1)
    half = jnp.pad(rep, [(0, 0)] * (cols.ndim - 1) + [(0, HEAD_DIM - C_PARTS * N_HEADS)])
    return jnp.concatenate([half, half], axis=-1)


def _bias_kernel(relt_ref, bkt_ref, out_ref, *, seq):
    h = pl.program_id(0)
    bkt = bkt_ref[...]
    w = jnp.zeros(bkt.shape, jnp.float32)
    for b in range(NUM_BUCKETS):
        w = jnp.where(bkt == b, relt_ref[h, b], w)
    x = jnp.broadcast_to(w, (BLOCK, bkt.shape[1]))
    y = pltpu.roll(x, 0, 1, stride=1, stride_axis=0)
    out_ref[0] = y[:, BLOCK:].T


def _t5_bucket(dist):
    max_exact = NUM_BUCKETS // 2
    is_small = dist < max_exact
    d = jnp.maximum(dist, 1).astype(jnp.float32)
    large = max_exact + (jnp.log(d / max_exact) / math.log(MAX_DISTANCE / max_exact)
                         * (NUM_BUCKETS - max_exact)).astype(jnp.int32)
    large = jnp.minimum(large, NUM_BUCKETS - 1)
    return jnp.where(is_small, dist, large)


def _bias_strips(rel_bias, seq):
    n = jnp.arange(seq + BLOCK, dtype=jnp.int32)
    bkt = _t5_bucket(jnp.clip(seq - n, 0, seq - 1))[None, :]
    return pl.pallas_call(
        functools.partial(_bias_kernel, seq=seq),
        grid=(N_HEADS,),
        in_specs=[
            pl.BlockSpec(memory_space=pltpu.SMEM),
            pl.BlockSpec((1, seq + BLOCK), lambda h: (0, 0)),
        ],
        out_specs=pl.BlockSpec((1, seq, BLOCK), lambda h: (h, 0, 0)),
        out_shape=jax.ShapeDtypeStruct((N_HEADS, seq, BLOCK), jnp.float32),
        compiler_params=pltpu.CompilerParams(dimension_semantics=("arbitrary",)),
        name="moba_bias",
    )(rel_bias.T, bkt)


def _iota(shape, axis):
    return lax.broadcasted_iota(jnp.int32, shape, axis)


def _split3(x):
    p1 = x.astype(jnp.bfloat16)
    r1 = x - p1.astype(jnp.float32)
    p2 = r1.astype(jnp.bfloat16)
    p3 = (r1 - p2.astype(jnp.float32)).astype(jnp.bfloat16)
    return p1, p2, p3


def _augment_kv(k_ref, vt_ref, k_extra, kp_sc, vt_sc):
    seq = k_ref.shape[1]
    lane = _iota((seq, LANES), 1)
    for h in range(HEADS_IN_STEP):
        pr, hh = divmod(h, HEADS_PER_PAIR)
        hmask = (lane >= hh * HEAD_DIM) & (lane < (hh + 1) * HEAD_DIM)
        kp_sc[h] = jnp.where(hmask, k_ref[0, :, pr * LANES:(pr + 1) * LANES], k_extra)
        vt_sc[h, :HEAD_DIM, :] = vt_ref[h * HEAD_DIM:(h + 1) * HEAD_DIM, :]
        vt_sc[h, HEAD_DIM:, :] = jnp.ones((V_ROWS - HEAD_DIM, seq), vt_sc.dtype)


class _Unit:
    def __init__(self, h, i, qat, tile_term):
        self.h, self.i, self.qat, self.tile_term = h, i, qat, tile_term
        self.s_ref = self.mxp = self.m = self.acc = None


def _score_tile(u, j, kp_sc):
    key_in_tile = _iota((SLAB, BLOCK), 0)
    query_in_tile = _iota((SLAB, BLOCK), 1)
    s = jnp.dot(kp_sc[u.h, j * BLOCK:(j + 1) * BLOCK, :], u.qat, preferred_element_type=jnp.float32)
    for r in range(0, BLOCK, SLAB):
        piece = s[r:r + SLAB, :]
        if u.tile_term is not None:
            piece = piece + u.tile_term(j, r)
        if j == u.i:
            piece = jnp.where(key_in_tile + r <= query_in_tile, piece, NEG)
        u.s_ref[j * BLOCK + r:j * BLOCK + r + SLAB, :] = piece
        t = jnp.max(piece.reshape(SLAB // SUBLANES, SUBLANES, BLOCK), axis=0)
        u.mxp = t if u.mxp is None else jnp.maximum(u.mxp, t)


def _value_tile(u, j, vt_sc):
    if u.m is None:
        u.m = jnp.max(u.mxp, axis=0, keepdims=True)
    e = jnp.concatenate(
        [jnp.exp(u.s_ref[j * BLOCK + r:j * BLOCK + r + SLAB, :] - u.m).astype(jnp.bfloat16)
         for r in range(0, BLOCK, SLAB)], axis=0)
    pv = jnp.dot(vt_sc[u.h, :, j * BLOCK:(j + 1) * BLOCK], e, preferred_element_type=jnp.float32)
    u.acc = pv if u.acc is None else u.acc + pv


def _run_units(nb, make_unit, s_sc, kp_sc, vt_sc, o_ref, unit_major_scores):
    groups = [[(pr * HEADS_PER_PAIR + hh, i) for i in (a, nb - 1 - a) for hh in range(HEADS_PER_PAIR)]
              for pr in range(PAIRS_PER_STEP) for a in range(nb // 2)]
    prev = []
    for g, grp in enumerate(groups + [None]):
        cur = [make_unit(h, i) for h, i in grp] if grp is not None else []
        for n, u in enumerate(cur):
            u.s_ref = s_sc.at[(g % 2) * UNITS_PER_GROUP + n]
        if unit_major_scores:
            s_items = [(u, t) for u in cur for t in range(u.i + 1)]
            v_items = [(u, t) for t in range(nb) for u in prev if t <= u.i]
            for n in range(max(len(s_items), len(v_items))):
                if n < len(s_items):
                    _score_tile(*s_items[n], kp_sc)
                if n < len(v_items):
                    _value_tile(*v_items[n], vt_sc)
        else:
            for t in range(nb):
                for u in cur:
                    if t <= u.i:
                        _score_tile(u, t, kp_sc)
                for u in prev:
                    if t <= u.i:
                        _value_tile(u, t, vt_sc)
        for u in prev:
            out = u.acc[:HEAD_DIM, :] * (1.0 / u.acc[HEAD_DIM:HEAD_DIM + 1, :])
            o_ref[u.h * HEAD_DIM:(u.h + 1) * HEAD_DIM,
                  u.i * BLOCK:(u.i + 1) * BLOCK] = out.astype(o_ref.dtype)
        prev = cur


def _scaled_qt(qt_ref, h, i):
    pr = h // HEADS_PER_PAIR
    q = qt_ref[pr * LANES:(pr + 1) * LANES, i * BLOCK:(i + 1) * BLOCK]
    return q * jnp.asarray(SCALE, q.dtype)


def _with_spare_rows(qt, hh, spare):
    head = qt[hh * HEAD_DIM:(hh + 1) * HEAD_DIM, :]
    spare = jnp.concatenate(
        [spare, jnp.zeros((HEAD_DIM - spare.shape[0], BLOCK), qt.dtype)], axis=0)
    return jnp.concatenate([head, spare] if hh == 0 else [spare, head], axis=0)


def _attn_scratch(seq):
    return [pltpu.VMEM((2 * UNITS_PER_GROUP, seq, BLOCK), jnp.float32),
            pltpu.VMEM((HEADS_IN_STEP, seq, LANES), jnp.bfloat16),
            pltpu.VMEM((HEADS_IN_STEP, V_ROWS, seq), jnp.bfloat16)]


def _moba_kernel(qt_ref, k_ref, vt_ref, bias_ref, o_ref, s_sc, kp_sc, vt_sc, *, seq):
    nb = seq // BLOCK
    key_blk = _iota((seq, LANES), 0) // BLOCK
    onehot = ((_iota((seq, LANES), 1) % HEAD_DIM) == key_blk).astype(k_ref.dtype)
    _augment_kv(k_ref, vt_ref, onehot, kp_sc, vt_sc)
    blk_row = _iota((BF16_ROWS, BLOCK), 0)
    feat_row = _iota((LANES, BLOCK), 0)

    def pair_kmean(pr):
        k = k_ref[0, :, pr * LANES:(pr + 1) * LANES].astype(jnp.float32)
        kmean = jnp.sum(k.reshape(nb, BLOCK, LANES), axis=1) * (1.0 / BLOCK)
        return jnp.concatenate([kmean, jnp.zeros((BF16_ROWS - nb, LANES), jnp.float32)], axis=0)

    kmeans = [pair_kmean(pr) for pr in range(PAIRS_PER_STEP)]

    def make_unit(h, i):
        pr, hh = divmod(h, HEADS_PER_PAIR)
        qt = _scaled_qt(qt_ref, h, i)
        pen = jnp.zeros((BF16_ROWS, BLOCK), qt.dtype)
        if i > TOPK:
            in_head = (feat_row >= hh * HEAD_DIM) & (feat_row < (hh + 1) * HEAD_DIM)
            q_head = jnp.where(in_head, qt, jnp.zeros_like(qt)).astype(jnp.float32)
            gate = jnp.dot(kmeans[pr], q_head, precision=lax.Precision.HIGHEST,
                           preferred_element_type=jnp.float32)
            rank = jnp.zeros((BF16_ROWS, BLOCK), jnp.int32)
            for jp in range(i):
                other = gate[jp:jp + 1, :]
                beats = (other > gate) | ((other == gate) & (blk_row > jp))
                rank = rank + beats.astype(jnp.int32)
            pen = jnp.where((blk_row < i) & (rank >= TOPK), NEG, 0.0).astype(qt.dtype)

        def tile_term(j, r):
            off = (nb - 1 - i + j) * BLOCK + r
            return bias_ref[h, off:off + SLAB, :]

        return _Unit(h, i, _with_spare_rows(qt, hh, pen), tile_term)

    _run_units(nb, make_unit, s_sc, kp_sc, vt_sc, o_ref, unit_major_scores=False)


def _moba(k3, qvt, bias):
    b, seq, _ = k3.shape
    return pl.pallas_call(
        functools.partial(_moba_kernel, seq=seq),
        grid=(N_STEPS, b),
        in_specs=[
            pl.BlockSpec((STEP_WIDTH, seq), lambda p, bi: (p, bi)),
            pl.BlockSpec((1, seq, STEP_WIDTH), lambda p, bi: (bi, 0, p)),
            pl.BlockSpec((STEP_WIDTH, seq), lambda p, bi: (2 * N_STEPS + p, bi)),
            pl.BlockSpec((HEADS_IN_STEP, seq, BLOCK), lambda p, bi: (p, 0, 0)),
        ],
        out_specs=pl.BlockSpec((STEP_WIDTH, seq), lambda p, bi: (p, bi)),
        out_shape=jax.ShapeDtypeStruct((GROUP_WIDTH, b * seq), jnp.bfloat16),
        scratch_shapes=_attn_scratch(seq),
        compiler_params=pltpu.CompilerParams(
            dimension_semantics=("arbitrary", "arbitrary"), vmem_limit_bytes=VMEM_LIMIT),
        name="moba_attn",
    )(qvt, k3, qvt, bias)


def _fox_kernel(qt_ref, k_ref, vt_ref, logf_ref, o_ref, ext_sc, s_sc, kp_sc, vt_sc, *, seq):
    nb = seq // BLOCK
    p = pl.program_id(1)

    def write_forget_lanes():
        tri = (_iota((BLOCK, BLOCK), 1) <= _iota((BLOCK, BLOCK), 0)).astype(jnp.bfloat16)
        lane = _iota((BLOCK, LANES), 1) % HEAD_DIM
        carry = jnp.zeros((1, LANES), jnp.float32)
        for ch in range(nb):
            parts = _split3(logf_ref[ch * BLOCK:(ch + 1) * BLOCK, :])
            cs = jnp.dot(tri, jnp.concatenate(parts, axis=1), preferred_element_type=jnp.float32)
            c = cs[:, :LANES] + cs[:, LANES:2 * LANES] + cs[:, 2 * LANES:] + carry
            carry = c[BLOCK - 1:BLOCK, :]
            p1, p2, p3 = _split3(-c)
            part = lane % C_PARTS
            ext = jnp.where(part == 0, p1, jnp.where(part == 1, p2, p3))
            ext_sc[ch * BLOCK:(ch + 1) * BLOCK, :] = jnp.where(
                lane < C_PARTS * N_HEADS, ext, jnp.zeros_like(ext))

    write_forget_lanes()
    _augment_kv(k_ref, vt_ref, ext_sc[...], kp_sc, vt_sc)
    n_spare = -(-C_PARTS * N_HEADS // BF16_ROWS) * BF16_ROWS
    row = _iota((n_spare, BLOCK), 0)

    def make_unit(h, i):
        qt = _scaled_qt(qt_ref, h, i)
        first = C_PARTS * (HEADS_IN_STEP * p + h)
        spare = jnp.where((row >= first) & (row < first + C_PARTS), 1.0, 0.0).astype(qt.dtype)
        return _Unit(h, i, _with_spare_rows(qt, h % HEADS_PER_PAIR, spare), None)

    _run_units(nb, make_unit, s_sc, kp_sc, vt_sc, o_ref, unit_major_scores=True)


def _fox(k3, qvt, logf3):
    b, seq, _ = k3.shape
    return pl.pallas_call(
        functools.partial(_fox_kernel, seq=seq),
        grid=(b, N_STEPS),
        in_specs=[
            pl.BlockSpec((STEP_WIDTH, seq), lambda bi, p: (N_STEPS + p, bi)),
            pl.BlockSpec((1, seq, STEP_WIDTH), lambda bi, p: (bi, 0, N_STEPS + p)),
            pl.BlockSpec((STEP_WIDTH, seq), lambda bi, p: (3 * N_STEPS + p, bi)),
            pl.BlockSpec((None, seq, LANES), lambda bi, p: (bi, 0, 0)),
        ],
        out_specs=pl.BlockSpec((STEP_WIDTH, seq), lambda bi, p: (p, bi)),
        out_shape=jax.ShapeDtypeStruct((GROUP_WIDTH, b * seq), jnp.bfloat16),
        scratch_shapes=[pltpu.VMEM((seq, LANES), jnp.bfloat16)] + _attn_scratch(seq),
        compiler_params=pltpu.CompilerParams(
            dimension_semantics=("arbitrary", "arbitrary"), vmem_limit_bytes=VMEM_LIMIT),
        name="fox_attn",
    )(qvt, k3, qvt, logf3)


def _dense_kernel(x_ref, ymt_ref, yft_ref, wo_ref, g_ref, wgu_ref, wd_ref, gfin_ref, o_ref, *, final):
    x1 = (x_ref[...]
          + lax.dot_general(ymt_ref[...], wo_ref[:GROUP_WIDTH, :], _TN,
                            preferred_element_type=jnp.float32)
          + lax.dot_general(yft_ref[...], wo_ref[GROUP_WIDTH:, :], _TN,
                            preferred_element_type=jnp.float32))
    h = _rms(x1, g_ref[...]).astype(jnp.bfloat16)
    acc = jnp.zeros_like(x1)
    for c in range(D_FF // FF_CHUNK):
        lo = c * FF_CHUNK
        gate = jnp.dot(h, wgu_ref[:, lo:lo + FF_CHUNK], preferred_element_type=jnp.float32)
        up = jnp.dot(h, wgu_ref[:, D_FF + lo:D_FF + lo + FF_CHUNK], preferred_element_type=jnp.float32)
        a = (gate / (1.0 + jnp.exp(-gate)) * up).astype(jnp.bfloat16)
        acc = acc + jnp.dot(a, wd_ref[lo:lo + FF_CHUNK, :], preferred_element_type=jnp.float32)
    acc = acc + x1
    if final:
        acc = _rms(acc, gfin_ref[...])
    o_ref[...] = acc


def _dense(x2d, ymt, yft, wo_all, g, wgu_all, wd_all, gfin, layer, final):
    t = x2d.shape[0]
    const = lambda shape: pl.BlockSpec(shape, lambda i: (0,) * len(shape),
                                       pipeline_mode=pl.Buffered(1))
    weights = lambda stacked: _layer_block(stacked, layer, pipeline_mode=pl.Buffered(1))
    return pl.pallas_call(
        functools.partial(_dense_kernel, final=final),
        grid=(t // TOKEN_TILE,),
        in_specs=[
            pl.BlockSpec((TOKEN_TILE, D_MODEL), lambda i: (i, 0)),
            pl.BlockSpec((GROUP_WIDTH, TOKEN_TILE), lambda i: (0, i)),
            pl.BlockSpec((GROUP_WIDTH, TOKEN_TILE), lambda i: (0, i)),
            weights(wo_all),
            const((1, D_MODEL)),
            weights(wgu_all),
            weights(wd_all),
            const((1, D_MODEL)),
        ],
        out_specs=pl.BlockSpec((TOKEN_TILE, D_MODEL), lambda i: (i, 0)),
        out_shape=jax.ShapeDtypeStruct((t, D_MODEL), jnp.float32),
        compiler_params=pltpu.CompilerParams(
            dimension_semantics=("arbitrary",), vmem_limit_bytes=VMEM_LIMIT),
        name="dense_ffn",
    )(x2d, ymt, yft, wo_all, g, wgu_all, wd_all, gfin)


def kernel(x, w_in, b_f, w_o, g_attn, w_gu, w_down, g_ffn, rel_bias, g_final):
    b, seq, d = x.shape
    depth = w_in.shape[0]
    assert d == D_MODEL and seq % BLOCK == 0 and (b * seq) % TOKEN_TILE == 0
    assert seq // BLOCK <= BF16_ROWS and C_PARTS * N_HEADS <= HEAD_DIM
    bf16 = jnp.bfloat16
    gw = GROUP_WIDTH
    bias = _bias_strips(rel_bias, seq)
    w_in16, w_o16, w_gu16, w_down16 = (w.astype(bf16) for w in (w_in, w_o, w_gu, w_down))
    w_f = _forget_lane_layout(w_in[:, :, 6 * gw:]).astype(bf16)
    b_lanes = _forget_lane_layout(b_f[:, None, :])
    x2d = x.reshape(b * seq, d)
    for layer in range(depth):
        k2, qvt, logf = _inproj(x2d, g_attn[layer][None, :], w_in16, layer, w_f[layer], b_lanes[layer])
        k3 = k2.reshape(b, seq, 2 * gw)
        ymt = _moba(k3, qvt, bias)
        yft = _fox(k3, qvt, logf.reshape(b, seq, LANES))
        x2d = _dense(x2d, ymt, yft, w_o16, g_ffn[layer][None, :], w_gu16, w_down16,
                     g_final[None, :], layer, final=(layer == depth - 1))
    return x2d.reshape(b, seq, d)
```
